```python
import jax, jax.numpy as jnp
from jax import lax
import numpy as np

D_MODEL = 1024
BATCH = 8
SEQ = 8192
DEPTH = 4

CTX_LEN = 256
GRID_W = 64
D_MIX = D_MODEL
D_HGRN = D_MIX // 2
HGRN_HEAD_DIM = 128
HGRN_HEADS = D_HGRN // HGRN_HEAD_DIM
HGRN_CHUNK = 16
D_POOL = D_MIX - D_HGRN
POOL_WINDOWS = (2, 4, 8, 16)
POOL_GROUPS = len(POOL_WINDOWS)
POOL_GROUP_DIM = D_POOL // POOL_GROUPS
D_IN = 5 * D_HGRN + D_POOL
D_FF_DENSE = 2816
N_EXPERTS = 8
TOP_K = 2
D_FF_EXPERT = 3584
N_ADA = 6
EPS = 1e-6

kernel_name = "hybrid_hgrn2_pool_moe_dit_trunk"


def rms_norm(x, w):
    xf = x.astype(jnp.float32)
    y = xf * lax.rsqrt(jnp.mean(xf * xf, axis=-1, keepdims=True) + EPS)
    return (y * w.astype(jnp.float32)).astype(x.dtype)


def modulate(h, shift, scale):
    return h * (1 + scale) + shift


def gla_chunk_scan(q, k, v, log_f, state0):
    B, L, H, DK = q.shape
    DV = v.shape[-1]
    C = HGRN_CHUNK
    nc = L // C

    def chunks(a):
        return a.reshape(B, nc, C, H, a.shape[-1]).transpose(1, 0, 3, 2, 4)

    incl = jnp.tril(jnp.ones((C, C), dtype=bool))[:, :, None]

    def step(state, blk):
        qb, kb, vb, gb = blk
        b = jnp.cumsum(gb, axis=2)
        rel = jnp.where(incl, b[:, :, :, None, :] - b[:, :, None, :, :], -jnp.inf)
        scores = jnp.einsum('bhtk,bhsk,bhtsk->bhts', qb, kb, jnp.exp(rel))
        o = (jnp.einsum('bhts,bhsv->bhtv', scores, vb)
             + jnp.einsum('bhtk,bhkv->bhtv', qb * jnp.exp(b), state))
        b_end = b[:, :, -1:, :]
        state = (jnp.exp(b_end[:, :, 0, :, None]) * state
                 + jnp.einsum('bhsk,bhsv->bhkv', kb * jnp.exp(b_end - b), vb))
        return state, o

    state, o = lax.scan(step, state0, tuple(map(chunks, (q, k, v, log_f))))
    return o.transpose(1, 0, 3, 2, 4).reshape(B, L, H, DV), state


def hgrn_inputs(z, lb):
    B, L = z.shape[:2]
    q, i, f_fwd, f_bwd, g = jnp.split(z, 5, axis=-1)
    heads = lambda a: a.reshape(B, L, HGRN_HEADS, HGRN_HEAD_DIM)
    log_fs, ks = [], []
    for d, zf in enumerate((f_fwd, f_bwd)):
        f = lb[d] + (1 - lb[d]) * jax.nn.sigmoid(zf)
        log_fs.append(heads(jnp.log(f)))
        ks.append(heads(1 - f))
    return heads(q) * HGRN_HEAD_DIM ** -0.5, heads(i), log_fs, ks, g


def hgrn_readout(o, g, w):
    B, L = o.shape[:2]
    o = o * lax.rsqrt(jnp.mean(o * o, axis=-1, keepdims=True) + EPS) * w.reshape(HGRN_HEADS, HGRN_HEAD_DIM)
    return o.reshape(B, L, D_HGRN) * jax.nn.silu(g)


def hgrn_mixer(z_ctx, z_lat, lb, out_norm_w):
    q_c, i_c, lf_c, k_c, g_c = hgrn_inputs(z_ctx, lb)
    q_l, i_l, lf_l, k_l, g_l = hgrn_inputs(z_lat, lb)
    B = z_lat.shape[0]
    state0 = jnp.zeros((B, HGRN_HEADS, HGRN_HEAD_DIM, HGRN_HEAD_DIM), jnp.float32)
    o_ctx, o_lat = 0.0, 0.0
    for d in range(2):
        rev = (lambda a: jnp.flip(a, axis=1)) if d else (lambda a: a)
        oc, s_ctx = gla_chunk_scan(rev(q_c), rev(k_c[d]), rev(i_c), rev(lf_c[d]), state0)
        ol, _ = gla_chunk_scan(rev(q_l), rev(k_l[d]), rev(i_l), rev(lf_l[d]), s_ctx)
        o_ctx = o_ctx + rev(oc)
        o_lat = o_lat + rev(ol)
    return hgrn_readout(o_ctx, g_c, out_norm_w), hgrn_readout(o_lat, g_l, out_norm_w)


def centred_window_sum(x, axis, k):
    n = x.shape[axis]
    lo, hi = k // 2, k - 1 - k // 2
    cs = jnp.cumsum(x, axis=axis)
    cs = jnp.concatenate([jnp.zeros_like(lax.slice_in_dim(cs, 0, 1, axis=axis)), cs], axis=axis)
    pos = np.arange(n)
    end = np.minimum(pos + hi + 1, n)
    start = np.maximum(pos - lo, 0)
    total = jnp.take(cs, end, axis=axis) - jnp.take(cs, start, axis=axis)
    return total, (end - start).astype(np.float32)


def pool_latent(v):
    B, T = v.shape[:2]
    rows = T // GRID_W
    grid = v.reshape(B, rows, GRID_W, POOL_GROUPS, POOL_GROUP_DIM)
    means = []
    for j, k in enumerate(POOL_WINDOWS):
        s_col, n_col = centred_window_sum(grid[:, :, :, j, :], 2, k)
        s, n_row = centred_window_sum(s_col, 1, k)
        means.append(s / (n_row[:, None] * n_col[None, :])[None, :, :, None])
    return jnp.stack(means, axis=3).reshape(B, T, POOL_GROUPS, POOL_GROUP_DIM) - v


def pool_context(v):
    means = []
    for j, k in enumerate(POOL_WINDOWS):
        s, n = centred_window_sum(v[:, :, j, :], 1, k)
        means.append(s / n[None, :, None])
    return jnp.stack(means, axis=2) - v


def pool_mixer(diff, w, scale):
    B, L = diff.shape[:2]
    return jnp.einsum('blgc,gcd->blgd', diff, w).reshape(B, L, D_POOL) * scale


def swiglu(h, wg, wu, wd):
    return (jax.nn.silu(h @ wg) * (h @ wu)) @ wd


def moe_swiglu(h, w_router, wg, wu, wd):
    logits = jnp.einsum('bld,de->ble', h.astype(jnp.float32), w_router.astype(jnp.float32))
    top_logits, top_idx = lax.top_k(logits, TOP_K)
    top_w = jax.nn.softmax(top_logits, axis=-1)
    combine = jnp.einsum('blk,blke->ble', top_w, jax.nn.one_hot(top_idx, N_EXPERTS, dtype=jnp.float32))
    out = jnp.zeros(h.shape, jnp.float32)
    for e in range(N_EXPERTS):
        out = out + combine[..., e:e + 1] * swiglu(h, wg[e], wu[e], wd[e]).astype(jnp.float32)
    return out


def setup_inputs(seed: int = 0) -> dict:
    key = jax.random.key(seed)
    ks = jax.random.split(key, 24)
    n_dense, n_moe = (DEPTH + 1) // 2, DEPTH // 2
    f32 = jnp.float32
    nrm = lambda k, shape: jax.random.normal(k, shape, f32)
    return {
        "x": nrm(ks[0], (BATCH, SEQ, D_MODEL)),
        "c": nrm(ks[1], (BATCH, D_MODEL)),
        "ctx": nrm(ks[2], (BATCH, CTX_LEN, D_MODEL)),
        "c_ctx": nrm(ks[3], (D_MODEL,)),
        "w_ada": nrm(ks[4], (DEPTH, D_MODEL, N_ADA * D_MODEL)) * 0.5 * D_MODEL ** -0.5,
        "b_ada": nrm(ks[5], (DEPTH, N_ADA * D_MODEL)) * 0.02,
        "norm_w": 1 + 0.1 * nrm(ks[6], (DEPTH, 2, D_MODEL)),
        "w_in": nrm(ks[7], (DEPTH, D_MODEL, D_IN)) * D_MODEL ** -0.5,
        "hgrn_lb_raw": 0.5 * nrm(ks[8], (DEPTH, 2, D_HGRN)),
        "hgrn_norm_w": 1 + 0.1 * nrm(ks[9], (DEPTH, D_HGRN)),
        "w_pool": nrm(ks[10], (DEPTH, POOL_GROUPS, POOL_GROUP_DIM, POOL_GROUP_DIM)) * POOL_GROUP_DIM ** -0.5,
        "pool_scale": 0.5 + 0.1 * nrm(ks[11], (DEPTH, D_POOL)),
        "w_out": nrm(ks[12], (DEPTH, D_MIX, D_MODEL)) * D_MIX ** -0.5,
        "ffn_wg": nrm(ks[13], (n_dense, D_MODEL, D_FF_DENSE)) * D_MODEL ** -0.5,
        "ffn_wu": nrm(ks[14], (n_dense, D_MODEL, D_FF_DENSE)) * D_MODEL ** -0.5,
        "ffn_wd": nrm(ks[15], (n_dense, D_FF_DENSE, D_MODEL)) * D_FF_DENSE ** -0.5,
        "router_w": nrm(ks[16], (n_moe, D_MODEL, N_EXPERTS)) * D_MODEL ** -0.5,
        "moe_wg": nrm(ks[17], (n_moe, N_EXPERTS, D_MODEL, D_FF_EXPERT)) * D_MODEL ** -0.5,
        "moe_wu": nrm(ks[18], (n_moe, N_EXPERTS, D_MODEL, D_FF_EXPERT)) * D_MODEL ** -0.5,
        "moe_wd": nrm(ks[19], (n_moe, N_EXPERTS, D_FF_EXPERT, D_MODEL)) * D_FF_EXPERT ** -0.5,
        "final_norm_w": 1 + 0.1 * nrm(ks[20], (D_MODEL,)),
    }


def reference(x, c, ctx, c_ctx, w_ada, b_ada, norm_w, w_in, hgrn_lb_raw, hgrn_norm_w, w_pool,
              pool_scale, w_out, ffn_wg, ffn_wu, ffn_wd, router_w, moe_wg, moe_wu, moe_wd,
              final_norm_w):
    f32 = jnp.float32
    B, T, D = x.shape
    Lc = ctx.shape[1]
    lb_cum = jnp.cumsum(jax.nn.softmax(hgrn_lb_raw.astype(f32), axis=0), axis=0)
    lower_bounds = lb_cum - lb_cum[:1]
    h_lat, h_ctx = x, ctx
    for l in range(DEPTH):
        last = l == DEPTH - 1
        mod_l = (jax.nn.silu(c) @ w_ada[l] + b_ada[l]).reshape(B, N_ADA, 1, D)
        mod_c = (jax.nn.silu(c_ctx) @ w_ada[l] + b_ada[l]).reshape(N_ADA, 1, 1, D)
        u_lat = modulate(rms_norm(h_lat, norm_w[l, 0]), mod_l[:, 0], mod_l[:, 1])
        u_ctx = modulate(rms_norm(h_ctx, norm_w[l, 0]), mod_c[0], mod_c[1])
        z = (jnp.concatenate([u_ctx, u_lat], axis=1) @ w_in[l]).astype(f32)
        z_ctx, z_lat = z[:, :Lc], z[:, Lc:]
        a_ctx, a_lat = hgrn_mixer(z_ctx[..., :5 * D_HGRN], z_lat[..., :5 * D_HGRN],
                                  lower_bounds[l], hgrn_norm_w[l])
        pv_lat = z_lat[..., 5 * D_HGRN:].reshape(B, T, POOL_GROUPS, POOL_GROUP_DIM)
        p_lat = pool_mixer(pool_latent(pv_lat), w_pool[l], pool_scale[l])
        y_lat = jnp.concatenate([a_lat, p_lat], axis=-1) @ w_out[l]
        h_lat = h_lat + (mod_l[:, 2] * y_lat).astype(h_lat.dtype)
        if not last:
            pv_ctx = z_ctx[..., 5 * D_HGRN:].reshape(B, Lc, POOL_GROUPS, POOL_GROUP_DIM)
            p_ctx = pool_mixer(pool_context(pv_ctx), w_pool[l], pool_scale[l])
            y_ctx = jnp.concatenate([a_ctx, p_ctx], axis=-1) @ w_out[l]
            h_ctx = h_ctx + (mod_c[2] * y_ctx).astype(h_ctx.dtype)
        v = modulate(rms_norm(h_lat, norm_w[l, 1]), mod_l[:, 3], mod_l[:, 4])
        if not last:
            v = jnp.concatenate([modulate(rms_norm(h_ctx, norm_w[l, 1]), mod_c[3], mod_c[4]), v], axis=1)
        if l % 2 == 0:
            y = swiglu(v, ffn_wg[l // 2], ffn_wu[l // 2], ffn_wd[l // 2])
        else:
            y = moe_swiglu(v, router_w[l // 2], moe_wg[l // 2], moe_wu[l // 2], moe_wd[l // 2])
        h_lat = h_lat + (mod_l[:, 5] * y[:, -T:]).astype(h_lat.dtype)
        if not last:
            h_ctx = h_ctx + (mod_c[5] * y[:, :Lc]).astype(h_ctx.dtype)
    return rms_norm(h_lat, final_norm_w)
```

```python
import functools

import jax
import jax.numpy as jnp
from jax import lax
from jax.experimental import pallas as pl
from jax.experimental.pallas import tpu as pltpu

F32 = jnp.float32
BF16 = jnp.bfloat16
I32 = jnp.int32
U32 = jnp.uint32

HEAD_DIM = 128
GRID_W = 64
POOL_WINDOWS = (2, 4, 8, 16)
N_ADA = 6
TOP_K = 2
EPS = 1e-6

LANES = 128
ROW_TILE = 256
HGRN_CHUNK = 128
EXPERT_TILE = 1024
HGRN_FAST_LIMIT = 160.0
VMEM_LIMIT_BYTES = 48 * 1024 * 1024

_NT = (((1,), (1,)), ((), ()))


def _params(*semantics):
    return pltpu.CompilerParams(dimension_semantics=semantics, vmem_limit_bytes=VMEM_LIMIT_BYTES)


def _dot(a, b):
    return jnp.dot(a, b, preferred_element_type=F32)


def _silu(x):
    return x * jax.nn.sigmoid(x)


def _rms(x, w):
    return x * lax.rsqrt(jnp.mean(x * x, axis=-1, keepdims=True) + EPS) * w


def _mod_row(tile, tiles_per_batch, ctx_tiles, batch):
    return jnp.where(tile % tiles_per_batch < ctx_tiles, batch, tile // tiles_per_batch)


def _mod_kernel(c_ref, w_ref, b_ref, o_ref):
    o_ref[0] = jnp.dot(_silu(c_ref[...]), w_ref[0], preferred_element_type=F32,
                       precision=lax.Precision.HIGHEST) + b_ref[0]


def _modulation(cond, w_ada, b_ada):
    depth, d, n_out = w_ada.shape
    rows = cond.shape[0]
    tn = 1024
    return pl.pallas_call(
        _mod_kernel,
        grid=(depth, n_out // tn),
        in_specs=[
            pl.BlockSpec((rows, d), lambda l, j: (0, 0)),
            pl.BlockSpec((1, d, tn), lambda l, j: (l, 0, j)),
            pl.BlockSpec((1, 1, tn), lambda l, j: (l, 0, j)),
        ],
        out_specs=pl.BlockSpec((1, rows, tn), lambda l, j: (l, 0, j)),
        out_shape=jax.ShapeDtypeStruct((depth, rows, n_out), F32),
        compiler_params=_params("arbitrary", "arbitrary"),
        name="modulation",
    )(cond, w_ada, b_ada.reshape(depth, 1, n_out))


def _mix_in_kernel(h_ref, mod_ref, nw_ref, w_ref, z_ref):
    u = _rms(h_ref[...], nw_ref[...]) * (1 + mod_ref[0, 1:2, :]) + mod_ref[0, 0:1, :]
    z_ref[...] = _dot(u.astype(BF16), w_ref[...])


def _mix_in(h, mods, norm_w, w_in, geo):
    n, d = h.shape
    d_in = w_in.shape[1]
    row = functools.partial(_mod_row, tiles_per_batch=geo["tpb"], ctx_tiles=geo["ctx_tiles"], batch=geo["batch"])
    return pl.pallas_call(
        _mix_in_kernel,
        grid=(n // ROW_TILE,),
        in_specs=[
            pl.BlockSpec((ROW_TILE, d), lambda i: (i, 0)),
            pl.BlockSpec((1, N_ADA, d), lambda i: (row(i), 0, 0)),
            pl.BlockSpec((1, d), lambda i: (0, 0)),
            pl.BlockSpec((d, d_in), lambda i: (0, 0)),
        ],
        out_specs=pl.BlockSpec((ROW_TILE, d_in), lambda i: (i, 0)),
        out_shape=jax.ShapeDtypeStruct((n, d_in), F32),
        compiler_params=_params("arbitrary"),
        name="mix_in",
    )(h, mods, norm_w.reshape(1, d), w_in)


def _split3(x):
    hi = x.astype(BF16)
    r1 = x - hi.astype(F32)
    mid = r1.astype(BF16)
    lo = (r1 - mid.astype(F32)).astype(BF16)
    return hi, mid, lo


def _hgrn_chunk(q_ref, v_ref, lg_ref, k_ref, o_ref, st_ref, b_ref, row0, *, reverse, fast, heads):
    c = HGRN_CHUNK
    rows = pl.ds(row0, c)
    t_idx = lax.broadcasted_iota(I32, (c, c), 0)
    s_idx = lax.broadcasted_iota(I32, (c, c), 1)
    incl = (s_idx >= t_idx) if reverse else (s_idx <= t_idx)
    tri = jnp.where(incl, 1.0, 0.0).astype(BF16)
    hi, mid, lo = _split3(lg_ref[rows, :])
    b = _dot(tri, hi) + _dot(tri, mid) + _dot(tri, lo)
    b_end = b[0:1, :] if reverse else b[c - 1:c, :]
    q = q_ref[rows, :] * (HEAD_DIM ** -0.5)
    k = k_ref[rows, :]
    v = v_ref[rows, :]

    if fast:
        r = 0.5 * b_end
        er = jnp.exp(r)
        qt = (q * jnp.exp(b - r)).astype(BF16)
        kt = (k * jnp.exp(r - b)).astype(BF16)
        for h in range(heads):
            hs = slice(h * HEAD_DIM, (h + 1) * HEAD_DIM)
            scores = lax.dot_general(qt[:, hs], kt[:, hs], _NT, preferred_element_type=F32)
            p = jnp.where(incl, scores, 0.0).astype(BF16)
            st = st_ref[h] * er[:, hs]
            o = _dot(p, v[:, hs].astype(BF16)) + lax.dot_general(
                qt[:, hs], st.astype(BF16), _NT, preferred_element_type=F32)
            o_ref[rows, hs] = o
            st_ref[h] = (st + _dot(v[:, hs].T.astype(BF16), kt[:, hs])) * er[:, hs]
    else:
        qd = (q * jnp.exp(b)).astype(BF16)
        ke = (k * jnp.exp(b_end - b)).astype(BF16)
        decay = jnp.exp(b_end)
        b_ref[...] = b
        for h in range(heads):
            hs = slice(h * HEAD_DIM, (h + 1) * HEAD_DIM)
            st = st_ref[h]
            o_ref[rows, hs] = lax.dot_general(qd[:, hs], st.astype(BF16), _NT, preferred_element_type=F32)
            st_ref[h] = st * decay[:, hs] + _dot(v[:, hs].T.astype(BF16), ke[:, hs])
        t_col = lax.broadcasted_iota(I32, (c, HEAD_DIM), 0)

        def key_step(s, carry):
            b_s = b_ref[pl.ds(s, 1), :]
            k_s = k_ref[pl.ds(row0 + s, 1), :]
            v_s = v_ref[pl.ds(row0 + s, 1), :]
            prod = q * jnp.exp(jnp.minimum(b_ref[...] - b_s, 0.0)) * k_s
            visible = (t_col <= s) if reverse else (t_col >= s)
            for h in range(heads):
                hs = slice(h * HEAD_DIM, (h + 1) * HEAD_DIM)
                w = jnp.sum(prod[:, hs], axis=-1, keepdims=True)
                o_ref[rows, hs] += jnp.where(visible, w, 0.0) * v_s[:, hs]
            return carry

        lax.fori_loop(0, c, key_step, 0)


def _hgrn_kernel(qf_ref, vf_ref, ff_ref, qb_ref, vb_ref, fb_ref, lb_ref, of_ref, ob_ref,
                 stf_ref, stb_ref, lgf_ref, lgb_ref, kf_ref, kb_ref, b_ref, *, heads):
    block = qf_ref.shape[0]
    n_chunks = block // HGRN_CHUNK

    @pl.when(pl.program_id(1) == 0)
    def _():
        stf_ref[...] = jnp.zeros_like(stf_ref)
        stb_ref[...] = jnp.zeros_like(stb_ref)

    worst = jnp.float32(0.0)
    for d, (f_ref, lg_ref, k_ref) in enumerate(((ff_ref, lgf_ref, kf_ref), (fb_ref, lgb_ref, kb_ref))):
        lb = lb_ref[d:d + 1, :]
        f = lb + (1 - lb) * jax.nn.sigmoid(f_ref[...])
        lg = jnp.log(f)
        lg_ref[...] = lg
        k_ref[...] = 1 - f
        chunk_decay = -jnp.sum(lg.reshape(n_chunks, HGRN_CHUNK, lg.shape[-1]), axis=1)
        worst = jnp.maximum(worst, jnp.max(chunk_decay))
    fast_ok = worst < HGRN_FAST_LIMIT

    def run(fast):
        def body(ci, carry):
            fwd_row = pl.multiple_of(ci * HGRN_CHUNK, HGRN_CHUNK)
            bwd_row = pl.multiple_of((n_chunks - 1 - ci) * HGRN_CHUNK, HGRN_CHUNK)
            _hgrn_chunk(qf_ref, vf_ref, lgf_ref, kf_ref, of_ref, stf_ref, b_ref, fwd_row,
                        reverse=False, fast=fast, heads=heads)
            _hgrn_chunk(qb_ref, vb_ref, lgb_ref, kb_ref, ob_ref, stb_ref, b_ref, bwd_row,
                        reverse=True, fast=fast, heads=heads)
            return carry

        lax.fori_loop(0, n_chunks, body, 0)

    @pl.when(fast_ok)
    def _():
        run(True)

    @pl.when(jnp.logical_not(fast_ok))
    def _():
        run(False)


def _hgrn(z, lower_bounds, geo):
    n = z.shape[0]
    d_h = lower_bounds.shape[-1]
    heads = d_h // HEAD_DIM
    nblk, ctx_blocks, batch = geo["tpb"], geo["ctx_tiles"], geo["batch"]

    def fwd_block(b, j):
        return b * nblk + j

    def bwd_block(b, j):
        return b * nblk + jnp.where(j < ctx_blocks, ctx_blocks - 1 - j, nblk - 1 - (j - ctx_blocks))

    def col(block_fn, c):
        return pl.BlockSpec((ROW_TILE, d_h), lambda b, j: (block_fn(b, j), c))

    state = pltpu.VMEM((heads, HEAD_DIM, HEAD_DIM), F32)
    rows = pltpu.VMEM((ROW_TILE, d_h), F32)
    return pl.pallas_call(
        functools.partial(_hgrn_kernel, heads=heads),
        grid=(batch, nblk),
        in_specs=[col(fwd_block, 0), col(fwd_block, 1), col(fwd_block, 2),
                  col(bwd_block, 0), col(bwd_block, 1), col(bwd_block, 3),
                  pl.BlockSpec((2, d_h), lambda b, j: (0, 0))],
        out_specs=[col(fwd_block, 0), col(bwd_block, 0)],
        out_shape=[jax.ShapeDtypeStruct((n, d_h), F32)] * 2,
        scratch_shapes=[state, state, rows, rows, rows, rows, pltpu.VMEM((HGRN_CHUNK, d_h), F32)],
        compiler_params=_params("arbitrary", "arbitrary"),
        name="hgrn",
    )(z, z, z, z, z, z, lower_bounds)


def _window_sum(x, k):
    n = x.shape[0]
    t = lax.broadcasted_iota(I32, x.shape, 0)

    def ahead(a, d):
        return jnp.where(t + d < n, pltpu.roll(a, n - d, axis=0), 0.0)

    def behind(a, d):
        return jnp.where(t >= d, pltpu.roll(a, d, axis=0), 0.0)

    fwd = x
    bwd = behind(x, 1)
    w = 1
    while 2 * w <= k // 2:
        fwd = fwd + ahead(fwd, w)
        bwd = bwd + behind(bwd, w)
        w *= 2
    return fwd + bwd


def _window_count(shape, n, k, offset=0):
    t = lax.broadcasted_iota(I32, shape, 0) + offset
    lo, hi = k // 2, k - 1 - k // 2
    return (jnp.minimum(t + hi + 1, n) - jnp.maximum(t - lo, 0)).astype(F32)


def _pool_group(x_ref, o_ref, cs_ref, k, ctx_len, grid_rows):
    lo, hi = k // 2, k - 1 - k // 2
    pad = POOL_WINDOWS[-1] // 2

    x = x_ref[0:ctx_len, :]
    mean = _window_sum(x, k) / _window_count(x.shape, ctx_len, k)
    o_ref[0:ctx_len, :] = (mean - x).astype(o_ref.dtype)

    zeros = jnp.zeros((pad * GRID_W, LANES), F32)
    cs_ref[0:pad * GRID_W, :] = zeros
    cs_ref[(pad + grid_rows) * GRID_W:(2 * pad + grid_rows) * GRID_W, :] = zeros

    def col_pass(r, carry):
        src = pl.multiple_of(ctx_len + r * GRID_W, GRID_W)
        dst = pl.multiple_of((pad + r) * GRID_W, GRID_W)
        cs_ref[pl.ds(dst, GRID_W), :] = _window_sum(x_ref[pl.ds(src, GRID_W), :], k)
        return carry

    lax.fori_loop(0, grid_rows, col_pass, 0)
    n_col = _window_count((GRID_W, LANES), GRID_W, k)

    def row_pass(r, carry):
        acc = jnp.zeros((GRID_W, LANES), F32)
        for dr in range(-lo, hi + 1):
            acc = acc + cs_ref[pl.ds(pl.multiple_of((pad + r + dr) * GRID_W, GRID_W), GRID_W), :]
        n_row = (jnp.minimum(r + hi + 1, grid_rows) - jnp.maximum(r - lo, 0)).astype(F32)
        src = pl.multiple_of(ctx_len + r * GRID_W, GRID_W)
        o_ref[pl.ds(src, GRID_W), :] = (acc / (n_row * n_col) - x_ref[pl.ds(src, GRID_W), :]).astype(o_ref.dtype)
        return carry

    lax.fori_loop(0, grid_rows, row_pass, 0)


def _pool_kernel(x_ref, o_ref, cs_ref, *, ctx_len, grid_rows):
    group = pl.program_id(1)
    for gi, k in enumerate(POOL_WINDOWS):
        @pl.when(group == gi)
        def _(k=k):
            _pool_group(x_ref, o_ref, cs_ref, k, ctx_len, grid_rows)


def _pool(z, geo):
    n, d_in = z.shape
    groups = len(POOL_WINDOWS)
    first = d_in // LANES - groups
    seq, ctx_len = geo["seq"], geo["ctx_len"]
    grid_rows = (seq - ctx_len) // GRID_W
    pad = POOL_WINDOWS[-1] // 2
    return pl.pallas_call(
        functools.partial(_pool_kernel, ctx_len=ctx_len, grid_rows=grid_rows),
        grid=(geo["batch"], groups),
        in_specs=[pl.BlockSpec((seq, LANES), lambda b, g: (b, first + g))],
        out_specs=pl.BlockSpec((seq, LANES), lambda b, g: (b, g)),
        out_shape=jax.ShapeDtypeStruct((n, groups * LANES), BF16),
        scratch_shapes=[pltpu.VMEM(((grid_rows + 2 * pad) * GRID_W, LANES), F32)],
        compiler_params=_params("arbitrary", "arbitrary"),
        name="pool",
    )(z)


def _pack_bf16_pairs(x):
    w = x.shape[1] // 2
    lo = lax.bitcast_convert_type(x[:, :w].astype(BF16).astype(F32), U32)
    hi = lax.bitcast_convert_type(x[:, w:].astype(BF16).astype(F32), U32)
    return (lo >> 16) | (hi & jnp.uint32(0xFFFF0000))


def _unpack_bf16_pairs(w):
    lo = lax.bitcast_convert_type(w << 16, F32).astype(BF16)
    hi = lax.bitcast_convert_type(w & jnp.uint32(0xFFFF0000), F32).astype(BF16)
    return lo, hi


def _route(v, rw_ref, carry_ref, n_exp):
    rows = v.shape[0]
    logits = jnp.dot(v, rw_ref[...], preferred_element_type=F32, precision=lax.Precision.HIGHEST)
    lane = lax.broadcasted_iota(I32, (rows, LANES), 1).astype(F32)
    neg = jnp.float32(-jnp.inf)
    logits = jnp.where(lane < n_exp, logits, neg)
    m1 = jnp.max(logits, axis=-1, keepdims=True)
    i1 = jnp.min(jnp.where(logits == m1, lane, float(LANES)), axis=-1, keepdims=True)
    rest = jnp.where(lane == i1, neg, logits)
    m2 = jnp.max(rest, axis=-1, keepdims=True)
    i2 = jnp.min(jnp.where(rest == m2, lane, float(LANES)), axis=-1, keepdims=True)
    e = jnp.exp(m2 - m1)
    w1 = 1.0 / (1.0 + e)
    w2 = e / (1.0 + e)
    chosen = jnp.where((lane == i1) | (lane == i2), 1.0, 0.0)
    t_idx = lax.broadcasted_iota(I32, (rows, rows), 0)
    s_idx = lax.broadcasted_iota(I32, (rows, rows), 1)
    before = jnp.where(s_idx < t_idx, 1.0, 0.0).astype(BF16)
    slots = _dot(before, chosen.astype(BF16)) + carry_ref[...]
    r1 = jnp.sum(jnp.where(lane == i1, slots, 0.0), axis=-1, keepdims=True)
    r2 = jnp.sum(jnp.where(lane == i2, slots, 0.0), axis=-1, keepdims=True)
    carry_ref[...] += jnp.sum(chosen, axis=0, keepdims=True)
    out = jnp.zeros((rows, LANES), F32)
    for idx, val in enumerate((i1, i2, r1, r2, w1, w2)):
        out = jnp.where(lane == idx, val, out)
    return out


def _mix_out_kernel(*refs, heads, groups, n_exp):
    (of_ref, ob_ref, g_ref, pd_ref, h_ref, mod_ref, hnw_ref, wp_ref, ps_ref, wo_ref, nw_ref) = refs[:11]
    if n_exp:
        rw_ref, h1_ref, v_ref, route_ref, cnt_ref, carry_ref = refs[11:]
    else:
        h1_ref, v_ref = refs[11:]
    d_h = heads * HEAD_DIM
    o = of_ref[...] + ob_ref[...]
    normed = []
    for hd in range(heads):
        oh = o[:, hd * HEAD_DIM:(hd + 1) * HEAD_DIM]
        normed.append(oh * lax.rsqrt(jnp.mean(oh * oh, axis=-1, keepdims=True) + EPS))
    a = jnp.concatenate(normed, axis=-1) * hnw_ref[...] * _silu(g_ref[...])
    y = _dot(a.astype(BF16), wo_ref[0:d_h, :])
    for gi in range(groups):
        gs = slice(gi * LANES, (gi + 1) * LANES)
        p = _dot(pd_ref[:, gs], wp_ref[gi]) * ps_ref[:, gs]
        y = y + _dot(p.astype(BF16), wo_ref[d_h + gi * LANES:d_h + (gi + 1) * LANES, :])
    h1 = h_ref[...] + mod_ref[0, 2:3, :] * y
    h1_ref[...] = h1
    v = _rms(h1, nw_ref[...]) * (1 + mod_ref[0, 4:5, :]) + mod_ref[0, 3:4, :]
    if n_exp:
        @pl.when(pl.program_id(0) == 0)
        def _():
            carry_ref[...] = jnp.zeros_like(carry_ref)

        v_ref[...] = _pack_bf16_pairs(v)
        route_ref[...] = _route(v, rw_ref, carry_ref, n_exp)
        cnt_ref[...] = jnp.broadcast_to(carry_ref[...], cnt_ref.shape)
    else:
        v_ref[...] = v.astype(BF16)


def _mix_out(o_fwd, o_bwd, z, pdiff, h, mods, hgrn_norm_w, w_pool, pool_scale, w_out, norm_w, router_w, geo):
    n, d = h.shape
    d_h = o_fwd.shape[1]
    groups = w_pool.shape[0]
    d_p = groups * LANES
    n_exp = 0 if router_w is None else router_w.shape[1]
    row = functools.partial(_mod_row, tiles_per_batch=geo["tpb"], ctx_tiles=geo["ctx_tiles"], batch=geo["batch"])
    tile = lambda width, c=0: pl.BlockSpec((ROW_TILE, width), lambda i: (i, c))
    whole = lambda shape: pl.BlockSpec(shape, lambda i: (0,) * len(shape))
    in_specs = [tile(d_h), tile(d_h), tile(d_h, 4), tile(d_p), tile(d),
                pl.BlockSpec((1, N_ADA, d), lambda i: (row(i), 0, 0)),
                whole((1, d_h)), whole((groups, LANES, LANES)), whole((1, d_p)), whole((d_h + d_p, d)),
                whole((1, d))]
    args = [o_fwd, o_bwd, z, pdiff, h, mods, hgrn_norm_w.reshape(1, d_h), w_pool, pool_scale.reshape(1, d_p),
            w_out, norm_w.reshape(1, d)]
    out_specs = [tile(d)]
    out_shape = [jax.ShapeDtypeStruct((n, d), F32)]
    scratch = []
    if n_exp:
        in_specs.append(whole((d, LANES)))
        args.append(jnp.pad(router_w, ((0, 0), (0, LANES - n_exp))))
        out_specs += [tile(d // 2), tile(LANES), whole((8, LANES))]
        out_shape += [jax.ShapeDtypeStruct((n, d // 2), U32), jax.ShapeDtypeStruct((n, LANES), F32),
                      jax.ShapeDtypeStruct((8, LANES), F32)]
        scratch = [pltpu.VMEM((1, LANES), F32)]
    else:
        out_specs.append(tile(d))
        out_shape.append(jax.ShapeDtypeStruct((n, d), BF16))
    return pl.pallas_call(
        functools.partial(_mix_out_kernel, heads=d_h // HEAD_DIM, groups=groups, n_exp=n_exp),
        grid=(n // ROW_TILE,),
        in_specs=in_specs,
        out_specs=out_specs,
        out_shape=out_shape,
        scratch_shapes=scratch,
        compiler_params=_params("arbitrary"),
        name="mix_out",
    )(*args)


def _residual_out(h_ref, mod_ref, y, fnw_ref, out_ref):
    h2 = h_ref[...] + mod_ref[0, 5:6, :] * y
    out_ref[...] = h2 if fnw_ref is None else _rms(h2, fnw_ref[...])


def _ffn_kernel(*refs, final):
    x_ref, h_ref, mod_ref, wg_ref, wu_ref, wd_ref = refs[:6]
    fnw_ref = refs[6] if final else None
    out_ref = refs[-1]
    x = x_ref[...]
    a = (_silu(_dot(x, wg_ref[...])) * _dot(x, wu_ref[...])).astype(BF16)
    _residual_out(h_ref, mod_ref, _dot(a, wd_ref[...]), fnw_ref, out_ref)


def _tile_maps(geo, final):
    tpb, ctx_tiles, batch = geo["tpb"], geo["ctx_tiles"], geo["batch"]
    if final:
        lat = tpb - ctx_tiles
        return (batch, lat), (lambda b, j: b * tpb + ctx_tiles + j), (lambda b, j: b * lat + j), batch * lat * ROW_TILE
    return (batch, tpb), (lambda b, j: b * tpb + j), (lambda b, j: b * tpb + j), batch * tpb * ROW_TILE


def _ffn(v, h, mods, wg, wu, wd, final_norm_w, geo):
    n, d = h.shape
    d_ff = wg.shape[1]
    final = final_norm_w is not None
    grid, in_tile, out_tile, out_rows = _tile_maps(geo, final)
    whole = lambda shape: pl.BlockSpec(shape, lambda b, j: (0,) * len(shape))
    in_specs = [pl.BlockSpec((ROW_TILE, d), lambda b, j: (in_tile(b, j), 0)),
                pl.BlockSpec((ROW_TILE, d), lambda b, j: (in_tile(b, j), 0)),
                pl.BlockSpec((1, N_ADA, d), lambda b, j: (b, 0, 0)) if final else
                pl.BlockSpec((1, N_ADA, d), lambda b, j: (jnp.where(j < geo["ctx_tiles"], geo["batch"], b), 0, 0)),
                whole((d, d_ff)), whole((d, d_ff)), whole((d_ff, d))]
    args = [v, h, mods, wg, wu, wd]
    if final:
        in_specs.append(whole((1, d)))
        args.append(final_norm_w.reshape(1, d))
    return pl.pallas_call(
        functools.partial(_ffn_kernel, final=final),
        grid=grid,
        in_specs=in_specs,
        out_specs=pl.BlockSpec((ROW_TILE, d), lambda b, j: (out_tile(b, j), 0)),
        out_shape=jax.ShapeDtypeStruct((out_rows, d), F32),
        compiler_params=_params("arbitrary", "arbitrary"),
        name="ffn",
    )(*args)


def _dispatch_kernel(pos_ref, v_ref, xs_in_ref, xs_ref, sem):
    del xs_in_ref
    i = pl.program_id(0)
    rows = v_ref.shape[0]

    def row_copy(r, k):
        return pltpu.make_async_copy(v_ref.at[pl.ds(r, 1), :], xs_ref.at[pl.ds(pos_ref[i, TOP_K * r + k], 1), :], sem)

    def issue(r, carry):
        for k in range(TOP_K):
            row_copy(r, k).start()
        return carry

    lax.fori_loop(0, rows, issue, 0)
    for _ in range(TOP_K):
        pltpu.make_async_copy(v_ref, xs_ref.at[pl.ds(0, rows), :], sem).wait()


def _dispatch(v_packed, pos, total_rows):
    n, w = v_packed.shape
    grid_spec = pltpu.PrefetchScalarGridSpec(
        num_scalar_prefetch=1,
        grid=(n // ROW_TILE,),
        in_specs=[pl.BlockSpec((ROW_TILE, w), lambda i, pos: (i, 0)), pl.BlockSpec(memory_space=pl.ANY)],
        out_specs=pl.BlockSpec(memory_space=pl.ANY),
        scratch_shapes=[pltpu.SemaphoreType.DMA(())],
    )
    return pl.pallas_call(
        _dispatch_kernel,
        grid_spec=grid_spec,
        out_shape=jax.ShapeDtypeStruct((total_rows, w), U32),
        input_output_aliases={2: 0},
        compiler_params=_params("arbitrary"),
        name="dispatch",
    )(pos, v_packed, jnp.zeros((total_rows, w), U32))


def _experts_kernel(te_ref, na_ref, xs_ref, wg_ref, wu_ref, wd_ref, y_ref, acc_ref, xlo_ref, xhi_ref):
    t, f = pl.program_id(0), pl.program_id(1)
    last = pl.num_programs(1) - 1
    half = xlo_ref.shape[1]
    active = t < na_ref[0]

    @pl.when(active & (f == 0))
    def _():
        xlo_ref[...], xhi_ref[...] = _unpack_bf16_pairs(xs_ref[...])
        acc_ref[...] = jnp.zeros_like(acc_ref)

    @pl.when(active)
    def _():
        xlo, xhi = xlo_ref[...], xhi_ref[...]
        g = _dot(xlo, wg_ref[0, 0:half, :]) + _dot(xhi, wg_ref[0, half:, :])
        u = _dot(xlo, wu_ref[0, 0:half, :]) + _dot(xhi, wu_ref[0, half:, :])
        acc_ref[...] += _dot((_silu(g) * u).astype(BF16), wd_ref[0])

    @pl.when(active & (f == last))
    def _():
        y_ref[...] = acc_ref[...]

    @pl.when(jnp.logical_not(active) & (f == last))
    def _():
        y_ref[...] = jnp.zeros_like(y_ref)


def _experts(xs, tile_expert, n_active, wg, wu, wd):
    rows, half = xs.shape
    n_exp, d, d_ff = wg.shape
    tf = 512
    n_f = d_ff // tf
    n_tiles = rows // EXPERT_TILE

    def live(t, na):
        return jnp.minimum(t, na[0] - 1)

    def up(t, f, te, na):
        return (te[live(t, na)], 0, jnp.where(t < na[0], f, n_f - 1))

    def down(t, f, te, na):
        return (te[live(t, na)], jnp.where(t < na[0], f, n_f - 1), 0)

    grid_spec = pltpu.PrefetchScalarGridSpec(
        num_scalar_prefetch=2,
        grid=(n_tiles, n_f),
        in_specs=[pl.BlockSpec((EXPERT_TILE, half), lambda t, f, te, na: (live(t, na), 0)),
                  pl.BlockSpec((1, d, tf), up), pl.BlockSpec((1, d, tf), up), pl.BlockSpec((1, tf, d), down)],
        out_specs=pl.BlockSpec((EXPERT_TILE, d), lambda t, f, te, na: (t, 0)),
        scratch_shapes=[pltpu.VMEM((EXPERT_TILE, d), F32), pltpu.VMEM((EXPERT_TILE, half), BF16),
                        pltpu.VMEM((EXPERT_TILE, half), BF16)],
    )
    return pl.pallas_call(
        _experts_kernel,
        grid_spec=grid_spec,
        out_shape=jax.ShapeDtypeStruct((rows, d), F32),
        compiler_params=_params("arbitrary", "arbitrary"),
        name="experts",
    )(tile_expert, n_active, xs, wg, wu, wd)


def _combine_kernel(*refs, in_tile, final):
    pos_ref, h_ref, route_ref, mod_ref = refs[:4]
    fnw_ref = refs[4] if final else None
    y_ref, out_ref, ybuf_ref, sem = refs[-4:]
    i = in_tile(pl.program_id(0), pl.program_id(1))
    rows = h_ref.shape[0]

    def issue(r, carry):
        for k in range(TOP_K):
            pltpu.make_async_copy(y_ref.at[pl.ds(pos_ref[i, TOP_K * r + k], 1), :],
                                  ybuf_ref.at[k, pl.ds(r, 1), :], sem).start()
        return carry

    lax.fori_loop(0, rows, issue, 0)
    for k in range(TOP_K):
        pltpu.make_async_copy(y_ref.at[pl.ds(0, rows), :], ybuf_ref.at[k], sem).wait()
    y = route_ref[:, 4:5] * ybuf_ref[0] + route_ref[:, 5:6] * ybuf_ref[1]
    _residual_out(h_ref, mod_ref, y, fnw_ref, out_ref)


def _combine(y_sorted, pos, route, h, mods, final_norm_w, geo):
    n, d = h.shape
    final = final_norm_w is not None
    grid, in_tile, out_tile, out_rows = _tile_maps(geo, final)
    in_specs = [pl.BlockSpec((ROW_TILE, d), lambda b, j, pos: (in_tile(b, j), 0)),
                pl.BlockSpec((ROW_TILE, LANES), lambda b, j, pos: (in_tile(b, j), 0)),
                pl.BlockSpec((1, N_ADA, d), lambda b, j, pos: (b, 0, 0)) if final else
                pl.BlockSpec((1, N_ADA, d),
                             lambda b, j, pos: (jnp.where(j < geo["ctx_tiles"], geo["batch"], b), 0, 0))]
    args = [h, route, mods]
    if final:
        in_specs.append(pl.BlockSpec((1, d), lambda b, j, pos: (0, 0)))
        args.append(final_norm_w.reshape(1, d))
    in_specs.append(pl.BlockSpec(memory_space=pl.ANY))
    args.append(y_sorted)
    grid_spec = pltpu.PrefetchScalarGridSpec(
        num_scalar_prefetch=1,
        grid=grid,
        in_specs=in_specs,
        out_specs=pl.BlockSpec((ROW_TILE, d), lambda b, j, pos: (out_tile(b, j), 0)),
        scratch_shapes=[pltpu.VMEM((TOP_K, ROW_TILE, d), F32), pltpu.SemaphoreType.DMA(())],
    )
    return pl.pallas_call(
        functools.partial(_combine_kernel, in_tile=in_tile, final=final),
        grid_spec=grid_spec,
        out_shape=jax.ShapeDtypeStruct((out_rows, d), F32),
        compiler_params=_params("arbitrary", "arbitrary"),
        name="combine",
    )(pos, *args)


def _routing_tables(route, counts, n_exp, n_tiles_max):
    experts = route[:, 0:TOP_K].astype(I32)
    slots = route[:, TOP_K:2 * TOP_K].astype(I32)
    cnt = counts[0, :n_exp].astype(I32)
    padded = (cnt + EXPERT_TILE - 1) // EXPERT_TILE * EXPERT_TILE
    ends = jnp.cumsum(padded)
    starts = ends - padded
    base = jnp.sum(jnp.where(experts[..., None] == jnp.arange(n_exp, dtype=I32), starts, 0), axis=-1)
    pos = (base + slots).reshape(-1, TOP_K * ROW_TILE)
    tile_start = jnp.arange(n_tiles_max, dtype=I32) * EXPERT_TILE
    tile_expert = jnp.minimum(jnp.sum(tile_start[:, None] >= ends[None, :], axis=-1), n_exp - 1).astype(I32)
    n_active = (ends[-1:] // EXPERT_TILE).astype(I32)
    return pos, tile_expert, n_active


def _moe(v_packed, route, counts, h, mods, wg, wu, wd, final_norm_w, geo):
    n = h.shape[0]
    n_exp = wg.shape[0]
    n_tiles_max = -(-n * TOP_K // EXPERT_TILE) + n_exp
    pos, tile_expert, n_active = _routing_tables(route, counts, n_exp, n_tiles_max)
    xs = _dispatch(v_packed, pos, n_tiles_max * EXPERT_TILE)
    ys = _experts(xs, tile_expert, n_active, wg, wu, wd)
    return _combine(ys, pos, route, h, mods, final_norm_w, geo)


def kernel(x, c, ctx, c_ctx, w_ada, b_ada, norm_w, w_in, hgrn_lb_raw, hgrn_norm_w, w_pool, pool_scale, w_out,
           ffn_wg, ffn_wu, ffn_wd, router_w, moe_wg, moe_wu, moe_wd, final_norm_w):
    batch, seq_len, d = x.shape
    ctx_len = ctx.shape[1]
    depth = w_in.shape[0]
    seq = ctx_len + seq_len
    assert ctx_len % ROW_TILE == 0 and seq_len % ROW_TILE == 0 and seq_len % GRID_W == 0
    geo = dict(batch=batch, seq=seq, ctx_len=ctx_len, tpb=seq // ROW_TILE, ctx_tiles=ctx_len // ROW_TILE)

    lb_cum = jnp.cumsum(jax.nn.softmax(hgrn_lb_raw.astype(F32), axis=0), axis=0)
    lower_bounds = lb_cum - lb_cum[:1]

    cond_rows = -(-(batch + 1) // 8) * 8
    cond = jnp.concatenate([c, c_ctx[None, :], jnp.zeros((cond_rows - batch - 1, d), F32)], axis=0)
    mods_all = _modulation(cond, w_ada, b_ada).reshape(depth, cond_rows, N_ADA, d)

    h = jnp.concatenate([ctx, x], axis=1).reshape(batch * seq, d)
    for l in range(depth):
        mods = mods_all[l]
        final_w = final_norm_w if l == depth - 1 else None
        z = _mix_in(h, mods, norm_w[l, 0], w_in[l].astype(BF16), geo)
        o_fwd, o_bwd = _hgrn(z, lower_bounds[l], geo)
        pdiff = _pool(z, geo)
        dense = l % 2 == 0
        outs = _mix_out(o_fwd, o_bwd, z, pdiff, h, mods, hgrn_norm_w[l], w_pool[l].astype(BF16), pool_scale[l],
                        w_out[l].astype(BF16), norm_w[l, 1], None if dense else router_w[l // 2], geo)
        if dense:
            h1, v = outs
            h = _ffn(v, h1, mods, ffn_wg[l // 2].astype(BF16), ffn_wu[l // 2].astype(BF16),
                     ffn_wd[l // 2].astype(BF16), final_w, geo)
        else:
            h1, v_packed, route, counts = outs
            h = _moe(v_packed, route, counts, h1, mods, moe_wg[l // 2].astype(BF16), moe_wu[l // 2].astype(BF16),
                     moe_wd[l // 2].astype(BF16), final_w, geo)
    return h.reshape(batch, seq_len, d)
```

```python
import functools

import jax
import jax.numpy as jnp
from jax import lax
from jax.experimental import pallas as pl
from jax.experimental.pallas import tpu as pltpu

F32 = jnp.float32
BF16 = jnp.bfloat16
I32 = jnp.int32
U32 = jnp.uint32

HEAD_DIM = 128
GRID_W = 64
POOL_WINDOWS = (2, 4, 8, 16)
N_ADA = 6
TOP_K = 2
EPS = 1e-6

LANES = 128
ROW_TILE = 256
HGRN_CHUNK = 128
EXPERT_TILE = 1024
ROW_COPY_UNROLL = 8
HGRN_FAST_LIMIT = 160.0
VMEM_LIMIT_BYTES = 48 * 1024 * 1024

_NT = (((1,), (1,)), ((), ()))


def _params(*semantics):
    return pltpu.CompilerParams(dimension_semantics=semantics, vmem_limit_bytes=VMEM_LIMIT_BYTES)


def _dot(a, b):
    return jnp.dot(a, b, preferred_element_type=F32)


def _silu(x):
    return x * jax.nn.sigmoid(x)


def _rms(x, w):
    return x * lax.rsqrt(jnp.mean(x * x, axis=-1, keepdims=True) + EPS) * w


def _mod_row(tile, tiles_per_batch, ctx_tiles, batch):
    return jnp.where(tile % tiles_per_batch < ctx_tiles, batch, tile // tiles_per_batch)


def _mod_kernel(c_ref, w_ref, b_ref, o_ref):
    o_ref[0] = jnp.dot(_silu(c_ref[...]), w_ref[0], preferred_element_type=F32,
                       precision=lax.Precision.HIGHEST) + b_ref[0]


def _modulation(cond, w_ada, b_ada):
    depth, d, n_out = w_ada.shape
    rows = cond.shape[0]
    tn = 1024
    return pl.pallas_call(
        _mod_kernel,
        grid=(depth, n_out // tn),
        in_specs=[
            pl.BlockSpec((rows, d), lambda l, j: (0, 0)),
            pl.BlockSpec((1, d, tn), lambda l, j: (l, 0, j)),
            pl.BlockSpec((1, 1, tn), lambda l, j: (l, 0, j)),
        ],
        out_specs=pl.BlockSpec((1, rows, tn), lambda l, j: (l, 0, j)),
        out_shape=jax.ShapeDtypeStruct((depth, rows, n_out), F32),
        compiler_params=_params("arbitrary", "arbitrary"),
        name="modulation",
    )(cond, w_ada, b_ada.reshape(depth, 1, n_out))


def _mix_in_kernel(h_ref, mod_ref, nw_ref, w_ref, z_ref):
    u = _rms(h_ref[...], nw_ref[...]) * (1 + mod_ref[0, 1:2, :]) + mod_ref[0, 0:1, :]
    z_ref[...] = _dot(u.astype(BF16), w_ref[...])


def _mix_in(h, mods, norm_w, w_in, geo):
    n, d = h.shape
    d_in = w_in.shape[1]
    row = functools.partial(_mod_row, tiles_per_batch=geo["tpb"], ctx_tiles=geo["ctx_tiles"], batch=geo["batch"])
    return pl.pallas_call(
        _mix_in_kernel,
        grid=(n // ROW_TILE,),
        in_specs=[
            pl.BlockSpec((ROW_TILE, d), lambda i: (i, 0)),
            pl.BlockSpec((1, N_ADA, d), lambda i: (row(i), 0, 0)),
            pl.BlockSpec((1, d), lambda i: (0, 0)),
            pl.BlockSpec((d, d_in), lambda i: (0, 0)),
        ],
        out_specs=pl.BlockSpec((ROW_TILE, d_in), lambda i: (i, 0)),
        out_shape=jax.ShapeDtypeStruct((n, d_in), F32),
        compiler_params=_params("arbitrary"),
        name="mix_in",
    )(h, mods, norm_w.reshape(1, d), w_in)


def _split3(x):
    hi = x.astype(BF16)
    r1 = x - hi.astype(F32)
    mid = r1.astype(BF16)
    lo = (r1 - mid.astype(F32)).astype(BF16)
    return hi, mid, lo


def _hgrn_chunk(q_ref, v_ref, lg_ref, k_ref, o_ref, st_ref, b_ref, row0, *, c, reverse, fast, heads):
    rows = pl.ds(row0, c)
    t_idx = lax.broadcasted_iota(I32, (c, c), 0)
    s_idx = lax.broadcasted_iota(I32, (c, c), 1)
    incl = (s_idx >= t_idx) if reverse else (s_idx <= t_idx)
    tri = jnp.where(incl, 1.0, 0.0).astype(BF16)
    hi, mid, lo = _split3(lg_ref[rows, :])
    b = _dot(tri, hi) + _dot(tri, mid) + _dot(tri, lo)
    b_end = b[0:1, :] if reverse else b[c - 1:c, :]
    q = q_ref[rows, :] * (HEAD_DIM ** -0.5)
    k = k_ref[rows, :]
    v = v_ref[rows, :]

    if fast:
        r = 0.5 * b_end
        er = jnp.exp(r)
        qt = (q * jnp.exp(b - r)).astype(BF16)
        kt = (k * jnp.exp(r - b)).astype(BF16)
        for h in range(heads):
            hs = slice(h * HEAD_DIM, (h + 1) * HEAD_DIM)
            scores = lax.dot_general(qt[:, hs], kt[:, hs], _NT, preferred_element_type=F32)
            p = jnp.where(incl, scores, 0.0).astype(BF16)
            st = st_ref[h] * er[:, hs]
            o = _dot(p, v[:, hs].astype(BF16)) + lax.dot_general(
                qt[:, hs], st.astype(BF16), _NT, preferred_element_type=F32)
            o_ref[rows, hs] = o
            st_ref[h] = (st + _dot(v[:, hs].T.astype(BF16), kt[:, hs])) * er[:, hs]
    else:
        qd = (q * jnp.exp(b)).astype(BF16)
        ke = (k * jnp.exp(b_end - b)).astype(BF16)
        decay = jnp.exp(b_end)
        b_ref[...] = b
        for h in range(heads):
            hs = slice(h * HEAD_DIM, (h + 1) * HEAD_DIM)
            st = st_ref[h]
            o_ref[rows, hs] = lax.dot_general(qd[:, hs], st.astype(BF16), _NT, preferred_element_type=F32)
            st_ref[h] = st * decay[:, hs] + _dot(v[:, hs].T.astype(BF16), ke[:, hs])
        t_col = lax.broadcasted_iota(I32, (c, HEAD_DIM), 0)

        def key_step(s, carry):
            b_s = b_ref[pl.ds(s, 1), :]
            k_s = k_ref[pl.ds(row0 + s, 1), :]
            v_s = v_ref[pl.ds(row0 + s, 1), :]
            prod = q * jnp.exp(jnp.minimum(b_ref[...] - b_s, 0.0)) * k_s
            visible = (t_col <= s) if reverse else (t_col >= s)
            for h in range(heads):
                hs = slice(h * HEAD_DIM, (h + 1) * HEAD_DIM)
                w = jnp.sum(prod[:, hs], axis=-1, keepdims=True)
                o_ref[rows, hs] += jnp.where(visible, w, 0.0) * v_s[:, hs]
            return carry

        lax.fori_loop(0, c, key_step, 0)


def _hgrn_kernel(qf_ref, vf_ref, ff_ref, qb_ref, vb_ref, fb_ref, lb_ref, of_ref, ob_ref,
                 stf_ref, stb_ref, lgf_ref, lgb_ref, kf_ref, kb_ref, b_ref, *, heads):
    block = qf_ref.shape[0]
    half = HGRN_CHUNK // 2
    n_half = block // half

    @pl.when(pl.program_id(1) == 0)
    def _():
        stf_ref[...] = jnp.zeros_like(stf_ref)
        stb_ref[...] = jnp.zeros_like(stb_ref)

    worst_full = jnp.float32(0.0)
    worst_half = jnp.float32(0.0)
    for d, (f_ref, lg_ref, k_ref) in enumerate(((ff_ref, lgf_ref, kf_ref), (fb_ref, lgb_ref, kb_ref))):
        lb = lb_ref[d:d + 1, :]
        f = lb + (1 - lb) * jax.nn.sigmoid(f_ref[...])
        lg = jnp.log(f)
        lg_ref[...] = lg
        k_ref[...] = 1 - f
        half_decay = -jnp.sum(lg.reshape(n_half, half, lg.shape[-1]), axis=1)
        worst_half = jnp.maximum(worst_half, jnp.max(half_decay))
        for i in range(0, n_half, 2):
            worst_full = jnp.maximum(worst_full, jnp.max(half_decay[i:i + 1] + half_decay[i + 1:i + 2]))
    full_ok = worst_full < HGRN_FAST_LIMIT
    half_ok = jnp.logical_and(jnp.logical_not(full_ok), worst_half < HGRN_FAST_LIMIT)
    neither = jnp.logical_and(jnp.logical_not(full_ok), jnp.logical_not(worst_half < HGRN_FAST_LIMIT))

    def run(c, fast):
        n_chunks = block // c

        def body(ci, carry):
            fwd_row = ci * c
            bwd_row = (n_chunks - 1 - ci) * c
            if not isinstance(ci, int):
                fwd_row, bwd_row = pl.multiple_of(fwd_row, c), pl.multiple_of(bwd_row, c)
            _hgrn_chunk(qf_ref, vf_ref, lgf_ref, kf_ref, of_ref, stf_ref, b_ref, fwd_row,
                        c=c, reverse=False, fast=fast, heads=heads)
            _hgrn_chunk(qb_ref, vb_ref, lgb_ref, kb_ref, ob_ref, stb_ref, b_ref, bwd_row,
                        c=c, reverse=True, fast=fast, heads=heads)
            return carry

        if fast:
            for ci in range(n_chunks):
                body(ci, 0)
        else:
            lax.fori_loop(0, n_chunks, body, 0)

    @pl.when(full_ok)
    def _():
        run(HGRN_CHUNK, True)

    @pl.when(half_ok)
    def _():
        run(half, True)

    @pl.when(neither)
    def _():
        run(HGRN_CHUNK, False)


def _hgrn(z, lower_bounds, geo):
    n = z.shape[0]
    d_h = lower_bounds.shape[-1]
    heads = d_h // HEAD_DIM
    nblk, ctx_blocks, batch = geo["tpb"], geo["ctx_tiles"], geo["batch"]

    def fwd_block(b, j):
        return b * nblk + j

    def bwd_block(b, j):
        return b * nblk + jnp.where(j < ctx_blocks, ctx_blocks - 1 - j, nblk - 1 - (j - ctx_blocks))

    def col(block_fn, c):
        return pl.BlockSpec((ROW_TILE, d_h), lambda b, j: (block_fn(b, j), c))

    state = pltpu.VMEM((heads, HEAD_DIM, HEAD_DIM), F32)
    rows = pltpu.VMEM((ROW_TILE, d_h), F32)
    return pl.pallas_call(
        functools.partial(_hgrn_kernel, heads=heads),
        grid=(batch, nblk),
        in_specs=[col(fwd_block, 0), col(fwd_block, 1), col(fwd_block, 2),
                  col(bwd_block, 0), col(bwd_block, 1), col(bwd_block, 3),
                  pl.BlockSpec((2, d_h), lambda b, j: (0, 0))],
        out_specs=[col(fwd_block, 0), col(bwd_block, 0)],
        out_shape=[jax.ShapeDtypeStruct((n, d_h), F32)] * 2,
        scratch_shapes=[state, state, rows, rows, rows, rows, pltpu.VMEM((HGRN_CHUNK, d_h), F32)],
        compiler_params=_params("arbitrary", "arbitrary"),
        name="hgrn",
    )(z, z, z, z, z, z, lower_bounds)


def _window_sum(x, k):
    n = x.shape[0]
    t = lax.broadcasted_iota(I32, x.shape, 0)

    def ahead(a, d):
        return jnp.where(t + d < n, pltpu.roll(a, n - d, axis=0), 0.0)

    def behind(a, d):
        return jnp.where(t >= d, pltpu.roll(a, d, axis=0), 0.0)

    fwd = x
    bwd = behind(x, 1)
    w = 1
    while 2 * w <= k // 2:
        fwd = fwd + ahead(fwd, w)
        bwd = bwd + behind(bwd, w)
        w *= 2
    return fwd + bwd


def _window_count(shape, n, k, offset=0):
    t = lax.broadcasted_iota(I32, shape, 0) + offset
    lo, hi = k // 2, k - 1 - k // 2
    return (jnp.minimum(t + hi + 1, n) - jnp.maximum(t - lo, 0)).astype(F32)


def _pool_group(x_ref, o_ref, cs_ref, k, ctx_len, grid_rows):
    lo, hi = k // 2, k - 1 - k // 2
    pad = POOL_WINDOWS[-1] // 2

    x = x_ref[0:ctx_len, :]
    mean = _window_sum(x, k) / _window_count(x.shape, ctx_len, k)
    o_ref[0:ctx_len, :] = (mean - x).astype(o_ref.dtype)

    zeros = jnp.zeros((pad * GRID_W, LANES), F32)
    cs_ref[0:pad * GRID_W, :] = zeros
    cs_ref[(pad + grid_rows) * GRID_W:(2 * pad + grid_rows) * GRID_W, :] = zeros

    def col_pass(r, carry):
        src = pl.multiple_of(ctx_len + r * GRID_W, GRID_W)
        dst = pl.multiple_of((pad + r) * GRID_W, GRID_W)
        cs_ref[pl.ds(dst, GRID_W), :] = _window_sum(x_ref[pl.ds(src, GRID_W), :], k)
        return carry

    lax.fori_loop(0, grid_rows, col_pass, 0)
    n_col = _window_count((GRID_W, LANES), GRID_W, k)

    def row_pass(r, carry):
        acc = jnp.zeros((GRID_W, LANES), F32)
        for dr in range(-lo, hi + 1):
            acc = acc + cs_ref[pl.ds(pl.multiple_of((pad + r + dr) * GRID_W, GRID_W), GRID_W), :]
        n_row = (jnp.minimum(r + hi + 1, grid_rows) - jnp.maximum(r - lo, 0)).astype(F32)
        src = pl.multiple_of(ctx_len + r * GRID_W, GRID_W)
        o_ref[pl.ds(src, GRID_W), :] = (acc / (n_row * n_col) - x_ref[pl.ds(src, GRID_W), :]).astype(o_ref.dtype)
        return carry

    lax.fori_loop(0, grid_rows, row_pass, 0)


def _pool_kernel(x_ref, o_ref, cs_ref, *, ctx_len, grid_rows):
    group = pl.program_id(1)
    for gi, k in enumerate(POOL_WINDOWS):
        @pl.when(group == gi)
        def _(k=k):
            _pool_group(x_ref, o_ref, cs_ref, k, ctx_len, grid_rows)


def _pool(z, geo):
    n, d_in = z.shape
    groups = len(POOL_WINDOWS)
    first = d_in // LANES - groups
    seq, ctx_len = geo["seq"], geo["ctx_len"]
    grid_rows = (seq - ctx_len) // GRID_W
    pad = POOL_WINDOWS[-1] // 2
    return pl.pallas_call(
        functools.partial(_pool_kernel, ctx_len=ctx_len, grid_rows=grid_rows),
        grid=(geo["batch"], groups),
        in_specs=[pl.BlockSpec((seq, LANES), lambda b, g: (b, first + g))],
        out_specs=pl.BlockSpec((seq, LANES), lambda b, g: (b, g)),
        out_shape=jax.ShapeDtypeStruct((n, groups * LANES), BF16),
        scratch_shapes=[pltpu.VMEM(((grid_rows + 2 * pad) * GRID_W, LANES), F32)],
        compiler_params=_params("arbitrary", "arbitrary"),
        name="pool",
    )(z)


def _pack_bf16_pairs(x):
    w = x.shape[1] // 2
    lo = lax.bitcast_convert_type(x[:, :w].astype(BF16).astype(F32), U32)
    hi = lax.bitcast_convert_type(x[:, w:].astype(BF16).astype(F32), U32)
    return (lo >> 16) | (hi & jnp.uint32(0xFFFF0000))


def _unpack_bf16_pairs(w):
    lo = lax.bitcast_convert_type(w << 16, F32).astype(BF16)
    hi = lax.bitcast_convert_type(w & jnp.uint32(0xFFFF0000), F32).astype(BF16)
    return lo, hi


def _route(v, rw_ref, carry_ref, n_exp):
    rows = v.shape[0]
    v_hi = v.astype(BF16)
    v_lo = (v - v_hi.astype(F32)).astype(BF16)
    hi_prod = _dot(v_hi, rw_ref[...])
    logits = hi_prod[:, :LANES] + hi_prod[:, LANES:] + _dot(v_lo, rw_ref[:, :LANES])
    lane = lax.broadcasted_iota(I32, (rows, LANES), 1).astype(F32)
    neg = jnp.float32(-jnp.inf)
    logits = jnp.where(lane < n_exp, logits, neg)
    m1 = jnp.max(logits, axis=-1, keepdims=True)
    i1 = jnp.min(jnp.where(logits == m1, lane, float(LANES)), axis=-1, keepdims=True)
    rest = jnp.where(lane == i1, neg, logits)
    m2 = jnp.max(rest, axis=-1, keepdims=True)
    i2 = jnp.min(jnp.where(rest == m2, lane, float(LANES)), axis=-1, keepdims=True)
    e = jnp.exp(m2 - m1)
    w1 = 1.0 / (1.0 + e)
    w2 = e / (1.0 + e)
    chosen = jnp.where((lane == i1) | (lane == i2), 1.0, 0.0)
    t_idx = lax.broadcasted_iota(I32, (rows, rows), 0)
    s_idx = lax.broadcasted_iota(I32, (rows, rows), 1)
    before = jnp.where(s_idx < t_idx, 1.0, 0.0).astype(BF16)
    slots = _dot(before, chosen.astype(BF16)) + carry_ref[...]
    r1 = jnp.sum(jnp.where(lane == i1, slots, 0.0), axis=-1, keepdims=True)
    r2 = jnp.sum(jnp.where(lane == i2, slots, 0.0), axis=-1, keepdims=True)
    carry_ref[...] += jnp.sum(chosen, axis=0, keepdims=True)
    out = jnp.zeros((rows, LANES), F32)
    for idx, val in enumerate((i1, i2, r1, r2, w1, w2)):
        out = jnp.where(lane == idx, val, out)
    return out


def _mix_out_kernel(*refs, heads, groups, n_exp):
    (of_ref, ob_ref, g_ref, pd_ref, h_ref, mod_ref, hnw_ref, wp_ref, ps_ref, wo_ref, nw_ref) = refs[:11]
    if n_exp:
        rw_ref, h1_ref, v_ref, route_ref, cnt_ref, carry_ref = refs[11:]
    else:
        h1_ref, v_ref = refs[11:]
    d_h = heads * HEAD_DIM
    o = of_ref[...] + ob_ref[...]
    normed = []
    for hd in range(heads):
        oh = o[:, hd * HEAD_DIM:(hd + 1) * HEAD_DIM]
        normed.append(oh * lax.rsqrt(jnp.mean(oh * oh, axis=-1, keepdims=True) + EPS))
    a = jnp.concatenate(normed, axis=-1) * hnw_ref[...] * _silu(g_ref[...])
    y = _dot(a.astype(BF16), wo_ref[0:d_h, :])
    for gi in range(groups):
        gs = slice(gi * LANES, (gi + 1) * LANES)
        p = _dot(pd_ref[:, gs], wp_ref[gi]) * ps_ref[:, gs]
        y = y + _dot(p.astype(BF16), wo_ref[d_h + gi * LANES:d_h + (gi + 1) * LANES, :])
    h1 = h_ref[...] + mod_ref[0, 2:3, :] * y
    h1_ref[...] = h1
    v = _rms(h1, nw_ref[...]) * (1 + mod_ref[0, 4:5, :]) + mod_ref[0, 3:4, :]
    if n_exp:
        @pl.when(pl.program_id(0) == 0)
        def _():
            carry_ref[...] = jnp.zeros_like(carry_ref)

        v_ref[...] = _pack_bf16_pairs(v)
        route_ref[...] = _route(v, rw_ref, carry_ref, n_exp)
        cnt_ref[...] = jnp.broadcast_to(carry_ref[...], cnt_ref.shape)
    else:
        v_ref[...] = v.astype(BF16)


def _mix_out(o_fwd, o_bwd, z, pdiff, h, mods, hgrn_norm_w, w_pool, pool_scale, w_out, norm_w, router_w, geo):
    n, d = h.shape
    d_h = o_fwd.shape[1]
    groups = w_pool.shape[0]
    d_p = groups * LANES
    n_exp = 0 if router_w is None else router_w.shape[1]
    row = functools.partial(_mod_row, tiles_per_batch=geo["tpb"], ctx_tiles=geo["ctx_tiles"], batch=geo["batch"])
    tile = lambda width, c=0: pl.BlockSpec((ROW_TILE, width), lambda i: (i, c))
    whole = lambda shape: pl.BlockSpec(shape, lambda i: (0,) * len(shape))
    in_specs = [tile(d_h), tile(d_h), tile(d_h, 4), tile(d_p), tile(d),
                pl.BlockSpec((1, N_ADA, d), lambda i: (row(i), 0, 0)),
                whole((1, d_h)), whole((groups, LANES, LANES)), whole((1, d_p)), whole((d_h + d_p, d)),
                whole((1, d))]
    args = [o_fwd, o_bwd, z, pdiff, h, mods, hgrn_norm_w.reshape(1, d_h), w_pool, pool_scale.reshape(1, d_p),
            w_out, norm_w.reshape(1, d)]
    out_specs = [tile(d)]
    out_shape = [jax.ShapeDtypeStruct((n, d), F32)]
    scratch = []
    if n_exp:
        in_specs.append(whole((d, 2 * LANES)))
        rw = jnp.pad(router_w.astype(F32), ((0, 0), (0, LANES - n_exp)))
        rw_hi = rw.astype(BF16)
        args.append(jnp.concatenate([rw_hi, (rw - rw_hi.astype(F32)).astype(BF16)], axis=1))
        out_specs += [tile(d // 2), tile(LANES), whole((8, LANES))]
        out_shape += [jax.ShapeDtypeStruct((n, d // 2), U32), jax.ShapeDtypeStruct((n, LANES), F32),
                      jax.ShapeDtypeStruct((8, LANES), F32)]
        scratch = [pltpu.VMEM((1, LANES), F32)]
    else:
        out_specs.append(tile(d))
        out_shape.append(jax.ShapeDtypeStruct((n, d), BF16))
    return pl.pallas_call(
        functools.partial(_mix_out_kernel, heads=d_h // HEAD_DIM, groups=groups, n_exp=n_exp),
        grid=(n // ROW_TILE,),
        in_specs=in_specs,
        out_specs=out_specs,
        out_shape=out_shape,
        scratch_shapes=scratch,
        compiler_params=_params("arbitrary"),
        name="mix_out",
    )(*args)


def _residual_out(h_ref, mod_ref, y, fnw_ref, out_ref):
    h2 = h_ref[...] + mod_ref[0, 5:6, :] * y
    out_ref[...] = h2 if fnw_ref is None else _rms(h2, fnw_ref[...])


def _ffn_kernel(*refs, final):
    x_ref, h_ref, mod_ref, wg_ref, wu_ref, wd_ref = refs[:6]
    fnw_ref = refs[6] if final else None
    out_ref = refs[-1]
    x = x_ref[...]
    a = (_silu(_dot(x, wg_ref[...])) * _dot(x, wu_ref[...])).astype(BF16)
    _residual_out(h_ref, mod_ref, _dot(a, wd_ref[...]), fnw_ref, out_ref)


def _tile_maps(geo, final):
    tpb, ctx_tiles, batch = geo["tpb"], geo["ctx_tiles"], geo["batch"]
    if final:
        lat = tpb - ctx_tiles
        return (batch, lat), (lambda b, j: b * tpb + ctx_tiles + j), (lambda b, j: b * lat + j), batch * lat * ROW_TILE
    return (batch, tpb), (lambda b, j: b * tpb + j), (lambda b, j: b * tpb + j), batch * tpb * ROW_TILE


def _ffn(v, h, mods, wg, wu, wd, final_norm_w, geo):
    n, d = h.shape
    d_ff = wg.shape[1]
    final = final_norm_w is not None
    grid, in_tile, out_tile, out_rows = _tile_maps(geo, final)
    whole = lambda shape: pl.BlockSpec(shape, lambda b, j: (0,) * len(shape))
    in_specs = [pl.BlockSpec((ROW_TILE, d), lambda b, j: (in_tile(b, j), 0)),
                pl.BlockSpec((ROW_TILE, d), lambda b, j: (in_tile(b, j), 0)),
                pl.BlockSpec((1, N_ADA, d), lambda b, j: (b, 0, 0)) if final else
                pl.BlockSpec((1, N_ADA, d), lambda b, j: (jnp.where(j < geo["ctx_tiles"], geo["batch"], b), 0, 0)),
                whole((d, d_ff)), whole((d, d_ff)), whole((d_ff, d))]
    args = [v, h, mods, wg, wu, wd]
    if final:
        in_specs.append(whole((1, d)))
        args.append(final_norm_w.reshape(1, d))
    return pl.pallas_call(
        functools.partial(_ffn_kernel, final=final),
        grid=grid,
        in_specs=in_specs,
        out_specs=pl.BlockSpec((ROW_TILE, d), lambda b, j: (out_tile(b, j), 0)),
        out_shape=jax.ShapeDtypeStruct((out_rows, d), F32),
        compiler_params=_params("arbitrary", "arbitrary"),
        name="ffn",
    )(*args)


def _dispatch_kernel(pos_ref, v_ref, xs_in_ref, xs_ref, sem):
    del xs_in_ref
    i = pl.program_id(0)
    rows = v_ref.shape[0]

    def row_copy(r, k):
        return pltpu.make_async_copy(v_ref.at[pl.ds(r, 1), :], xs_ref.at[pl.ds(pos_ref[i, TOP_K * r + k], 1), :], sem)

    def issue(r, carry):
        for k in range(TOP_K):
            row_copy(r, k).start(priority=k)
        return carry

    lax.fori_loop(0, rows, issue, 0, unroll=ROW_COPY_UNROLL)
    for _ in range(TOP_K):
        pltpu.make_async_copy(v_ref, xs_ref.at[pl.ds(0, rows), :], sem).wait()


def _dispatch(v_packed, pos, total_rows):
    n, w = v_packed.shape
    grid_spec = pltpu.PrefetchScalarGridSpec(
        num_scalar_prefetch=1,
        grid=(n // ROW_TILE,),
        in_specs=[pl.BlockSpec((ROW_TILE, w), lambda i, pos: (i, 0)), pl.BlockSpec(memory_space=pl.ANY)],
        out_specs=pl.BlockSpec(memory_space=pl.ANY),
        scratch_shapes=[pltpu.SemaphoreType.DMA(())],
    )
    return pl.pallas_call(
        _dispatch_kernel,
        grid_spec=grid_spec,
        out_shape=jax.ShapeDtypeStruct((total_rows, w), U32),
        input_output_aliases={2: 0},
        compiler_params=_params("arbitrary"),
        name="dispatch",
    )(pos, v_packed, jnp.zeros((total_rows, w), U32))


def _experts_kernel(te_ref, na_ref, xs_ref, wg_ref, wu_ref, wd_ref, y_ref, acc_ref, xlo_ref, xhi_ref):
    t, f = pl.program_id(0), pl.program_id(1)
    last = pl.num_programs(1) - 1
    half = xlo_ref.shape[1]
    active = t < na_ref[0]

    @pl.when(active & (f == 0))
    def _():
        xlo_ref[...], xhi_ref[...] = _unpack_bf16_pairs(xs_ref[...])
        acc_ref[...] = jnp.zeros_like(acc_ref)

    @pl.when(active)
    def _():
        xlo, xhi = xlo_ref[...], xhi_ref[...]
        g = _dot(xlo, wg_ref[0, 0:half, :]) + _dot(xhi, wg_ref[0, half:, :])
        u = _dot(xlo, wu_ref[0, 0:half, :]) + _dot(xhi, wu_ref[0, half:, :])
        acc_ref[...] += _dot((_silu(g) * u).astype(BF16), wd_ref[0])

    @pl.when(active & (f == last))
    def _():
        y_ref[...] = acc_ref[...]

    @pl.when(jnp.logical_not(active) & (f == last))
    def _():
        y_ref[...] = jnp.zeros_like(y_ref)


def _experts(xs, tile_expert, n_active, wg, wu, wd):
    rows, half = xs.shape
    n_exp, d, d_ff = wg.shape
    tf = 512
    n_f = d_ff // tf
    n_tiles = rows // EXPERT_TILE

    def live(t, na):
        return jnp.minimum(t, na[0] - 1)

    def up(t, f, te, na):
        return (te[live(t, na)], 0, jnp.where(t < na[0], f, n_f - 1))

    def down(t, f, te, na):
        return (te[live(t, na)], jnp.where(t < na[0], f, n_f - 1), 0)

    grid_spec = pltpu.PrefetchScalarGridSpec(
        num_scalar_prefetch=2,
        grid=(n_tiles, n_f),
        in_specs=[pl.BlockSpec((EXPERT_TILE, half), lambda t, f, te, na: (live(t, na), 0)),
                  pl.BlockSpec((1, d, tf), up), pl.BlockSpec((1, d, tf), up), pl.BlockSpec((1, tf, d), down)],
        out_specs=pl.BlockSpec((EXPERT_TILE, d), lambda t, f, te, na: (t, 0)),
        scratch_shapes=[pltpu.VMEM((EXPERT_TILE, d), F32), pltpu.VMEM((EXPERT_TILE, half), BF16),
                        pltpu.VMEM((EXPERT_TILE, half), BF16)],
    )
    return pl.pallas_call(
        _experts_kernel,
        grid_spec=grid_spec,
        out_shape=jax.ShapeDtypeStruct((rows, d), F32),
        compiler_params=_params("arbitrary", "arbitrary"),
        name="experts",
    )(tile_expert, n_active, xs, wg, wu, wd)


def _combine_kernel(*refs, in_tile, final):
    pos_ref, h_ref, route_ref, mod_ref = refs[:4]
    fnw_ref = refs[4] if final else None
    y_ref, out_ref, ybuf_ref, sem = refs[-4:]
    i = in_tile(pl.program_id(0), pl.program_id(1))
    rows = h_ref.shape[0]

    def issue(r, carry):
        for k in range(TOP_K):
            pltpu.make_async_copy(y_ref.at[pl.ds(pos_ref[i, TOP_K * r + k], 1), :],
                                  ybuf_ref.at[k, pl.ds(r, 1), :], sem).start(priority=k)
        return carry

    lax.fori_loop(0, rows, issue, 0, unroll=ROW_COPY_UNROLL)
    for k in range(TOP_K):
        pltpu.make_async_copy(y_ref.at[pl.ds(0, rows), :], ybuf_ref.at[k], sem).wait()
    y = route_ref[:, 4:5] * ybuf_ref[0] + route_ref[:, 5:6] * ybuf_ref[1]
    _residual_out(h_ref, mod_ref, y, fnw_ref, out_ref)


def _combine(y_sorted, pos, route, h, mods, final_norm_w, geo):
    n, d = h.shape
    final = final_norm_w is not None
    grid, in_tile, out_tile, out_rows = _tile_maps(geo, final)
    in_specs = [pl.BlockSpec((ROW_TILE, d), lambda b, j, pos: (in_tile(b, j), 0)),
                pl.BlockSpec((ROW_TILE, LANES), lambda b, j, pos: (in_tile(b, j), 0)),
                pl.BlockSpec((1, N_ADA, d), lambda b, j, pos: (b, 0, 0)) if final else
                pl.BlockSpec((1, N_ADA, d),
                             lambda b, j, pos: (jnp.where(j < geo["ctx_tiles"], geo["batch"], b), 0, 0))]
    args = [h, route, mods]
    if final:
        in_specs.append(pl.BlockSpec((1, d), lambda b, j, pos: (0, 0)))
        args.append(final_norm_w.reshape(1, d))
    in_specs.append(pl.BlockSpec(memory_space=pl.ANY))
    args.append(y_sorted)
    grid_spec = pltpu.PrefetchScalarGridSpec(
        num_scalar_prefetch=1,
        grid=grid,
        in_specs=in_specs,
        out_specs=pl.BlockSpec((ROW_TILE, d), lambda b, j, pos: (out_tile(b, j), 0)),
        scratch_shapes=[pltpu.VMEM((TOP_K, ROW_TILE, d), F32), pltpu.SemaphoreType.DMA(())],
    )
    return pl.pallas_call(
        functools.partial(_combine_kernel, in_tile=in_tile, final=final),
        grid_spec=grid_spec,
        out_shape=jax.ShapeDtypeStruct((out_rows, d), F32),
        compiler_params=_params("arbitrary", "arbitrary"),
        name="combine",
    )(pos, *args)


def _routing_tables(route, counts, n_exp, n_tiles_max):
    experts = route[:, 0:TOP_K].astype(I32)
    slots = route[:, TOP_K:2 * TOP_K].astype(I32)
    cnt = counts[0, :n_exp].astype(I32)
    padded = (cnt + EXPERT_TILE - 1) // EXPERT_TILE * EXPERT_TILE
    ends = jnp.cumsum(padded)
    starts = ends - padded
    base = jnp.sum(jnp.where(experts[..., None] == jnp.arange(n_exp, dtype=I32), starts, 0), axis=-1)
    pos = (base + slots).reshape(-1, TOP_K * ROW_TILE)
    tile_start = jnp.arange(n_tiles_max, dtype=I32) * EXPERT_TILE
    tile_expert = jnp.minimum(jnp.sum(tile_start[:, None] >= ends[None, :], axis=-1), n_exp - 1).astype(I32)
    n_active = (ends[-1:] // EXPERT_TILE).astype(I32)
    return pos, tile_expert, n_active


def _moe(v_packed, route, counts, h, mods, wg, wu, wd, final_norm_w, geo):
    n = h.shape[0]
    n_exp = wg.shape[0]
    n_tiles_max = -(-n * TOP_K // EXPERT_TILE) + n_exp
    pos, tile_expert, n_active = _routing_tables(route, counts, n_exp, n_tiles_max)
    xs = _dispatch(v_packed, pos, n_tiles_max * EXPERT_TILE)
    ys = _experts(xs, tile_expert, n_active, wg, wu, wd)
    return _combine(ys, pos, route, h, mods, final_norm_w, geo)


def kernel(x, c, ctx, c_ctx, w_ada, b_ada, norm_w, w_in, hgrn_lb_raw, hgrn_norm_w, w_pool, pool_scale, w_out,
           ffn_wg, ffn_wu, ffn_wd, router_w, moe_wg, moe_wu, moe_wd, final_norm_w):
    batch, seq_len, d = x.shape
    ctx_len = ctx.shape[1]
    depth = w_in.shape[0]
    seq = ctx_len + seq_len
    assert ctx_len % ROW_TILE == 0 and seq_len % ROW_TILE == 0 and seq_len % GRID_W == 0
    geo = dict(batch=batch, seq=seq, ctx_len=ctx_len, tpb=seq // ROW_TILE, ctx_tiles=ctx_len // ROW_TILE)

    lb_cum = jnp.cumsum(jax.nn.softmax(hgrn_lb_raw.astype(F32), axis=0), axis=0)
    lower_bounds = lb_cum - lb_cum[:1]

    cond_rows = -(-(batch + 1) // 8) * 8
    cond = jnp.concatenate([c, c_ctx[None, :], jnp.zeros((cond_rows - batch - 1, d), F32)], axis=0)
    mods_all = _modulation(cond, w_ada, b_ada).reshape(depth, cond_rows, N_ADA, d)

    h = jnp.concatenate([ctx, x], axis=1).reshape(batch * seq, d)
    for l in range(depth):
        mods = mods_all[l]
        final_w = final_norm_w if l == depth - 1 else None
        z = _mix_in(h, mods, norm_w[l, 0], w_in[l].astype(BF16), geo)
        o_fwd, o_bwd = _hgrn(z, lower_bounds[l], geo)
        pdiff = _pool(z, geo)
        dense = l % 2 == 0
        outs = _mix_out(o_fwd, o_bwd, z, pdiff, h, mods, hgrn_norm_w[l], w_pool[l].astype(BF16), pool_scale[l],
                        w_out[l].astype(BF16), norm_w[l, 1], None if dense else router_w[l // 2], geo)
        if dense:
            h1, v = outs
            h = _ffn(v, h1, mods, ffn_wg[l // 2].astype(BF16), ffn_wu[l // 2].astype(BF16),
                     ffn_wd[l // 2].astype(BF16), final_w, geo)
        else:
            h1, v_packed, route, counts = outs
            h = _moe(v_packed, route, counts, h1, mods, moe_wg[l // 2].astype(BF16), moe_wu[l // 2].astype(BF16),
                     moe_wd[l // 2].astype(BF16), final_w, geo)
    return h.reshape(batch, seq_len, d)
```

```python
import functools

import jax
import jax.numpy as jnp
from jax import lax
from jax.experimental import pallas as pl
from jax.experimental.pallas import tpu as pltpu

F32 = jnp.float32
BF16 = jnp.bfloat16
I32 = jnp.int32
U32 = jnp.uint32

HEAD_DIM = 128
GRID_W = 64
POOL_WINDOWS = (2, 4, 8, 16)
N_ADA = 6
TOP_K = 2
EPS = 1e-6

LANES = 128
ROW_TILE = 256
HGRN_CHUNK = 128
EXPERT_TILE = 1024
HGRN_FAST_LIMIT = 160.0
VMEM_LIMIT_BYTES = 48 * 1024 * 1024

_NT = (((1,), (1,)), ((), ()))


def _params(*semantics):
    return pltpu.CompilerParams(dimension_semantics=semantics, vmem_limit_bytes=VMEM_LIMIT_BYTES)


def _dot(a, b):
    return jnp.dot(a, b, preferred_element_type=F32)


def _silu(x):
    return x * jax.nn.sigmoid(x)


def _rms(x, w):
    return x * lax.rsqrt(jnp.mean(x * x, axis=-1, keepdims=True) + EPS) * w


def _mod_row(tile, tiles_per_batch, ctx_tiles, batch):
    return jnp.where(tile % tiles_per_batch < ctx_tiles, batch, tile // tiles_per_batch)


def _mod_kernel(c_ref, w_ref, b_ref, o_ref):
    o_ref[0] = jnp.dot(_silu(c_ref[...]), w_ref[0], preferred_element_type=F32,
                       precision=lax.Precision.HIGHEST) + b_ref[0]


def _modulation(cond, w_ada, b_ada):
    depth, d, n_out = w_ada.shape
    rows = cond.shape[0]
    tn = 1024
    return pl.pallas_call(
        _mod_kernel,
        grid=(depth, n_out // tn),
        in_specs=[
            pl.BlockSpec((rows, d), lambda l, j: (0, 0)),
            pl.BlockSpec((1, d, tn), lambda l, j: (l, 0, j)),
            pl.BlockSpec((1, 1, tn), lambda l, j: (l, 0, j)),
        ],
        out_specs=pl.BlockSpec((1, rows, tn), lambda l, j: (l, 0, j)),
        out_shape=jax.ShapeDtypeStruct((depth, rows, n_out), F32),
        compiler_params=_params("arbitrary", "arbitrary"),
        name="modulation",
    )(cond, w_ada, b_ada.reshape(depth, 1, n_out))


def _mix_in_kernel(h_ref, mod_ref, nw_ref, w_ref, z_ref):
    u = _rms(h_ref[...], nw_ref[...]) * (1 + mod_ref[0, 1:2, :]) + mod_ref[0, 0:1, :]
    z_ref[...] = _dot(u.astype(BF16), w_ref[...])


def _mix_in(h, mods, norm_w, w_in, geo):
    n, d = h.shape
    d_in = w_in.shape[1]
    row = functools.partial(_mod_row, tiles_per_batch=geo["tpb"], ctx_tiles=geo["ctx_tiles"], batch=geo["batch"])
    return pl.pallas_call(
        _mix_in_kernel,
        grid=(n // ROW_TILE,),
        in_specs=[
            pl.BlockSpec((ROW_TILE, d), lambda i: (i, 0)),
            pl.BlockSpec((1, N_ADA, d), lambda i: (row(i), 0, 0)),
            pl.BlockSpec((1, d), lambda i: (0, 0)),
            pl.BlockSpec((d, d_in), lambda i: (0, 0)),
        ],
        out_specs=pl.BlockSpec((ROW_TILE, d_in), lambda i: (i, 0)),
        out_shape=jax.ShapeDtypeStruct((n, d_in), F32),
        compiler_params=_params("arbitrary"),
        name="mix_in",
    )(h, mods, norm_w.reshape(1, d), w_in)


def _split3(x):
    hi = x.astype(BF16)
    r1 = x - hi.astype(F32)
    mid = r1.astype(BF16)
    lo = (r1 - mid.astype(F32)).astype(BF16)
    return hi, mid, lo


def _chunk_cumsum(lg_ref, cum_ref, *, c, reverse):
    n = lg_ref.shape[0]
    t_idx = lax.broadcasted_iota(I32, (n, n), 0)
    s_idx = lax.broadcasted_iota(I32, (n, n), 1)
    incl = (s_idx >= t_idx) if reverse else (s_idx <= t_idx)
    same_chunk = (t_idx // c) == (s_idx // c)
    tri = jnp.where(jnp.logical_and(incl, same_chunk), 1.0, 0.0).astype(BF16)
    hi, mid, lo = _split3(lg_ref[...])
    cum_ref[...] = _dot(tri, hi) + _dot(tri, mid) + _dot(tri, lo)


def _hgrn_chunk(q_ref, v_ref, cum_ref, k_ref, o_ref, st_ref, row0, *, c, reverse, fast, heads):
    rows = pl.ds(row0, c)
    t_idx = lax.broadcasted_iota(I32, (c, c), 0)
    s_idx = lax.broadcasted_iota(I32, (c, c), 1)
    incl = (s_idx >= t_idx) if reverse else (s_idx <= t_idx)
    b = cum_ref[rows, :]
    b_end = b[0:1, :] if reverse else b[c - 1:c, :]
    q = q_ref[rows, :] * (HEAD_DIM ** -0.5)
    k = k_ref[rows, :]
    v = v_ref[rows, :]

    if fast:
        r = 0.5 * b_end
        er = jnp.exp(r)
        qt = (q * jnp.exp(b - r)).astype(BF16)
        kt = (k * jnp.exp(r - b)).astype(BF16)
        for h in range(heads):
            hs = slice(h * HEAD_DIM, (h + 1) * HEAD_DIM)
            scores = lax.dot_general(qt[:, hs], kt[:, hs], _NT, preferred_element_type=F32)
            p = jnp.where(incl, scores, 0.0).astype(BF16)
            st = st_ref[h] * er[:, hs]
            o = _dot(p, v[:, hs].astype(BF16)) + lax.dot_general(
                qt[:, hs], st.astype(BF16), _NT, preferred_element_type=F32)
            o_ref[rows, hs] = o
            st_ref[h] = (st + _dot(v[:, hs].T.astype(BF16), kt[:, hs])) * er[:, hs]
    else:
        qd = (q * jnp.exp(b)).astype(BF16)
        ke = (k * jnp.exp(b_end - b)).astype(BF16)
        decay = jnp.exp(b_end)
        for h in range(heads):
            hs = slice(h * HEAD_DIM, (h + 1) * HEAD_DIM)
            st = st_ref[h]
            o_ref[rows, hs] = lax.dot_general(qd[:, hs], st.astype(BF16), _NT, preferred_element_type=F32)
            st_ref[h] = st * decay[:, hs] + _dot(v[:, hs].T.astype(BF16), ke[:, hs])
        t_col = lax.broadcasted_iota(I32, (c, HEAD_DIM), 0)

        def key_step(s, carry):
            b_s = cum_ref[pl.ds(row0 + s, 1), :]
            k_s = k_ref[pl.ds(row0 + s, 1), :]
            v_s = v_ref[pl.ds(row0 + s, 1), :]
            prod = q * jnp.exp(jnp.minimum(b - b_s, 0.0)) * k_s
            visible = (t_col <= s) if reverse else (t_col >= s)
            for h in range(heads):
                hs = slice(h * HEAD_DIM, (h + 1) * HEAD_DIM)
                w = jnp.sum(prod[:, hs], axis=-1, keepdims=True)
                o_ref[rows, hs] += jnp.where(visible, w, 0.0) * v_s[:, hs]
            return carry

        lax.fori_loop(0, c, key_step, 0)


def _hgrn_kernel(qf_ref, vf_ref, ff_ref, qb_ref, vb_ref, fb_ref, lb_ref, of_ref, ob_ref,
                 stf_ref, stb_ref, lgf_ref, lgb_ref, kf_ref, kb_ref, cumf_ref, cumb_ref, *, heads):
    block = qf_ref.shape[0]
    half = HGRN_CHUNK // 2
    n_half = block // half

    @pl.when(pl.program_id(1) == 0)
    def _():
        stf_ref[...] = jnp.zeros_like(stf_ref)
        stb_ref[...] = jnp.zeros_like(stb_ref)

    worst_full = jnp.float32(0.0)
    worst_half = jnp.float32(0.0)
    for d, (f_ref, lg_ref, k_ref) in enumerate(((ff_ref, lgf_ref, kf_ref), (fb_ref, lgb_ref, kb_ref))):
        lb = lb_ref[d:d + 1, :]
        f = lb + (1 - lb) * jax.nn.sigmoid(f_ref[...])
        lg = jnp.log(f)
        lg_ref[...] = lg
        k_ref[...] = 1 - f
        half_decay = -jnp.sum(lg.reshape(n_half, half, lg.shape[-1]), axis=1)
        worst_half = jnp.maximum(worst_half, jnp.max(half_decay))
        for i in range(0, n_half, 2):
            worst_full = jnp.maximum(worst_full, jnp.max(half_decay[i:i + 1] + half_decay[i + 1:i + 2]))
    full_ok = worst_full < HGRN_FAST_LIMIT
    half_ok = jnp.logical_and(jnp.logical_not(full_ok), worst_half < HGRN_FAST_LIMIT)
    neither = jnp.logical_and(jnp.logical_not(full_ok), jnp.logical_not(worst_half < HGRN_FAST_LIMIT))

    def run(c, fast):
        n_chunks = block // c
        _chunk_cumsum(lgf_ref, cumf_ref, c=c, reverse=False)
        _chunk_cumsum(lgb_ref, cumb_ref, c=c, reverse=True)

        def body(ci, carry):
            fwd_row = ci * c
            bwd_row = (n_chunks - 1 - ci) * c
            if not isinstance(ci, int):
                fwd_row, bwd_row = pl.multiple_of(fwd_row, c), pl.multiple_of(bwd_row, c)
            _hgrn_chunk(qf_ref, vf_ref, cumf_ref, kf_ref, of_ref, stf_ref, fwd_row,
                        c=c, reverse=False, fast=fast, heads=heads)
            _hgrn_chunk(qb_ref, vb_ref, cumb_ref, kb_ref, ob_ref, stb_ref, bwd_row,
                        c=c, reverse=True, fast=fast, heads=heads)
            return carry

        if fast:
            for ci in range(n_chunks):
                body(ci, 0)
        else:
            lax.fori_loop(0, n_chunks, body, 0)

    @pl.when(full_ok)
    def _():
        run(HGRN_CHUNK, True)

    @pl.when(half_ok)
    def _():
        run(half, True)

    @pl.when(neither)
    def _():
        run(HGRN_CHUNK, False)


def _hgrn(z, lower_bounds, geo):
    n = z.shape[0]
    d_h = lower_bounds.shape[-1]
    heads = d_h // HEAD_DIM
    nblk, ctx_blocks, batch = geo["tpb"], geo["ctx_tiles"], geo["batch"]

    def fwd_block(b, j):
        return b * nblk + j

    def bwd_block(b, j):
        return b * nblk + jnp.where(j < ctx_blocks, ctx_blocks - 1 - j, nblk - 1 - (j - ctx_blocks))

    def col(block_fn, c):
        return pl.BlockSpec((ROW_TILE, d_h), lambda b, j: (block_fn(b, j), c))

    state = pltpu.VMEM((heads, HEAD_DIM, HEAD_DIM), F32)
    rows = pltpu.VMEM((ROW_TILE, d_h), F32)
    return pl.pallas_call(
        functools.partial(_hgrn_kernel, heads=heads),
        grid=(batch, nblk),
        in_specs=[col(fwd_block, 0), col(fwd_block, 1), col(fwd_block, 2),
                  col(bwd_block, 0), col(bwd_block, 1), col(bwd_block, 3),
                  pl.BlockSpec((2, d_h), lambda b, j: (0, 0))],
        out_specs=[col(fwd_block, 0), col(bwd_block, 0)],
        out_shape=[jax.ShapeDtypeStruct((n, d_h), F32)] * 2,
        scratch_shapes=[state, state, rows, rows, rows, rows, rows, rows],
        compiler_params=_params("arbitrary", "arbitrary"),
        name="hgrn",
    )(z, z, z, z, z, z, lower_bounds)


def _window_sum(x, k):
    n = x.shape[0]
    t = lax.broadcasted_iota(I32, x.shape, 0)

    def ahead(a, d):
        return jnp.where(t + d < n, pltpu.roll(a, n - d, axis=0), 0.0)

    def behind(a, d):
        return jnp.where(t >= d, pltpu.roll(a, d, axis=0), 0.0)

    fwd = x
    bwd = behind(x, 1)
    w = 1
    while 2 * w <= k // 2:
        fwd = fwd + ahead(fwd, w)
        bwd = bwd + behind(bwd, w)
        w *= 2
    return fwd + bwd


def _window_count(shape, n, k, offset=0):
    t = lax.broadcasted_iota(I32, shape, 0) + offset
    lo, hi = k // 2, k - 1 - k // 2
    return (jnp.minimum(t + hi + 1, n) - jnp.maximum(t - lo, 0)).astype(F32)


def _pool_group(x_ref, o_ref, cs_ref, k, ctx_len, grid_rows):
    lo, hi = k // 2, k - 1 - k // 2
    pad = POOL_WINDOWS[-1] // 2

    x = x_ref[0:ctx_len, :]
    mean = _window_sum(x, k) / _window_count(x.shape, ctx_len, k)
    o_ref[0:ctx_len, :] = (mean - x).astype(o_ref.dtype)

    zeros = jnp.zeros((pad * GRID_W, LANES), F32)
    cs_ref[0:pad * GRID_W, :] = zeros
    cs_ref[(pad + grid_rows) * GRID_W:(2 * pad + grid_rows) * GRID_W, :] = zeros

    def col_pass(r, carry):
        src = pl.multiple_of(ctx_len + r * GRID_W, GRID_W)
        dst = pl.multiple_of((pad + r) * GRID_W, GRID_W)
        cs_ref[pl.ds(dst, GRID_W), :] = _window_sum(x_ref[pl.ds(src, GRID_W), :], k)
        return carry

    lax.fori_loop(0, grid_rows, col_pass, 0)
    n_col = _window_count((GRID_W, LANES), GRID_W, k)

    def row_pass(r, carry):
        acc = jnp.zeros((GRID_W, LANES), F32)
        for dr in range(-lo, hi + 1):
            acc = acc + cs_ref[pl.ds(pl.multiple_of((pad + r + dr) * GRID_W, GRID_W), GRID_W), :]
        n_row = (jnp.minimum(r + hi + 1, grid_rows) - jnp.maximum(r - lo, 0)).astype(F32)
        src = pl.multiple_of(ctx_len + r * GRID_W, GRID_W)
        o_ref[pl.ds(src, GRID_W), :] = (acc / (n_row * n_col) - x_ref[pl.ds(src, GRID_W), :]).astype(o_ref.dtype)
        return carry

    lax.fori_loop(0, grid_rows, row_pass, 0)


def _pool_kernel(x_ref, o_ref, cs_ref, *, ctx_len, grid_rows):
    group = pl.program_id(1)
    for gi, k in enumerate(POOL_WINDOWS):
        @pl.when(group == gi)
        def _(k=k):
            _pool_group(x_ref, o_ref, cs_ref, k, ctx_len, grid_rows)


def _pool(z, geo):
    n, d_in = z.shape
    groups = len(POOL_WINDOWS)
    first = d_in // LANES - groups
    seq, ctx_len = geo["seq"], geo["ctx_len"]
    grid_rows = (seq - ctx_len) // GRID_W
    pad = POOL_WINDOWS[-1] // 2
    return pl.pallas_call(
        functools.partial(_pool_kernel, ctx_len=ctx_len, grid_rows=grid_rows),
        grid=(geo["batch"], groups),
        in_specs=[pl.BlockSpec((seq, LANES), lambda b, g: (b, first + g))],
        out_specs=pl.BlockSpec((seq, LANES), lambda b, g: (b, g)),
        out_shape=jax.ShapeDtypeStruct((n, groups * LANES), BF16),
        scratch_shapes=[pltpu.VMEM(((grid_rows + 2 * pad) * GRID_W, LANES), F32)],
        compiler_params=_params("arbitrary", "arbitrary"),
        name="pool",
    )(z)


def _pack_bf16_pairs(x):
    w = x.shape[1] // 2
    lo = lax.bitcast_convert_type(x[:, :w].astype(BF16).astype(F32), U32)
    hi = lax.bitcast_convert_type(x[:, w:].astype(BF16).astype(F32), U32)
    return (lo >> 16) | (hi & jnp.uint32(0xFFFF0000))


def _unpack_bf16_pairs(w):
    lo = lax.bitcast_convert_type(w << 16, F32).astype(BF16)
    hi = lax.bitcast_convert_type(w & jnp.uint32(0xFFFF0000), F32).astype(BF16)
    return lo, hi


def _route(v, rw_ref, carry_ref, n_exp):
    rows = v.shape[0]
    v_hi = v.astype(BF16)
    v_lo = (v - v_hi.astype(F32)).astype(BF16)
    hi_prod = _dot(v_hi, rw_ref[...])
    logits = hi_prod[:, :LANES] + hi_prod[:, LANES:] + _dot(v_lo, rw_ref[:, :LANES])
    lane = lax.broadcasted_iota(I32, (rows, LANES), 1).astype(F32)
    neg = jnp.float32(-jnp.inf)
    logits = jnp.where(lane < n_exp, logits, neg)
    m1 = jnp.max(logits, axis=-1, keepdims=True)
    i1 = jnp.min(jnp.where(logits == m1, lane, float(LANES)), axis=-1, keepdims=True)
    rest = jnp.where(lane == i1, neg, logits)
    m2 = jnp.max(rest, axis=-1, keepdims=True)
    i2 = jnp.min(jnp.where(rest == m2, lane, float(LANES)), axis=-1, keepdims=True)
    e = jnp.exp(m2 - m1)
    w1 = 1.0 / (1.0 + e)
    w2 = e / (1.0 + e)
    chosen = jnp.where((lane == i1) | (lane == i2), 1.0, 0.0)
    t_idx = lax.broadcasted_iota(I32, (rows, rows), 0)
    s_idx = lax.broadcasted_iota(I32, (rows, rows), 1)
    before = jnp.where(s_idx < t_idx, 1.0, 0.0).astype(BF16)
    slots = _dot(before, chosen.astype(BF16)) + carry_ref[...]
    r1 = jnp.sum(jnp.where(lane == i1, slots, 0.0), axis=-1, keepdims=True)
    r2 = jnp.sum(jnp.where(lane == i2, slots, 0.0), axis=-1, keepdims=True)
    carry_ref[...] += jnp.sum(chosen, axis=0, keepdims=True)
    out = jnp.zeros((rows, LANES), F32)
    for idx, val in enumerate((i1, i2, r1, r2, w1, w2)):
        out = jnp.where(lane == idx, val, out)
    return out


def _mix_out_kernel(*refs, heads, groups, n_exp):
    (of_ref, ob_ref, g_ref, pd_ref, h_ref, mod_ref, hnw_ref, wp_ref, ps_ref, wo_ref, nw_ref) = refs[:11]
    if n_exp:
        rw_ref, h1_ref, v_ref, route_ref, cnt_ref, carry_ref = refs[11:]
    else:
        h1_ref, v_ref = refs[11:]
    d_h = heads * HEAD_DIM
    o = of_ref[...] + ob_ref[...]
    normed = []
    for hd in range(heads):
        oh = o[:, hd * HEAD_DIM:(hd + 1) * HEAD_DIM]
        normed.append(oh * lax.rsqrt(jnp.mean(oh * oh, axis=-1, keepdims=True) + EPS))
    a = jnp.concatenate(normed, axis=-1) * hnw_ref[...] * _silu(g_ref[...])
    y = _dot(a.astype(BF16), wo_ref[0:d_h, :])
    for gi in range(groups):
        gs = slice(gi * LANES, (gi + 1) * LANES)
        p = _dot(pd_ref[:, gs], wp_ref[gi]) * ps_ref[:, gs]
        y = y + _dot(p.astype(BF16), wo_ref[d_h + gi * LANES:d_h + (gi + 1) * LANES, :])
    h1 = h_ref[...] + mod_ref[0, 2:3, :] * y
    h1_ref[...] = h1
    v = _rms(h1, nw_ref[...]) * (1 + mod_ref[0, 4:5, :]) + mod_ref[0, 3:4, :]
    if n_exp:
        @pl.when(pl.program_id(0) == 0)
        def _():
            carry_ref[...] = jnp.zeros_like(carry_ref)

        v_ref[...] = _pack_bf16_pairs(v)
        route_ref[...] = _route(v, rw_ref, carry_ref, n_exp)
        cnt_ref[...] = jnp.broadcast_to(carry_ref[...], cnt_ref.shape)
    else:
        v_ref[...] = v.astype(BF16)


def _mix_out(o_fwd, o_bwd, z, pdiff, h, mods, hgrn_norm_w, w_pool, pool_scale, w_out, norm_w, router_w, geo):
    n, d = h.shape
    d_h = o_fwd.shape[1]
    groups = w_pool.shape[0]
    d_p = groups * LANES
    n_exp = 0 if router_w is None else router_w.shape[1]
    row = functools.partial(_mod_row, tiles_per_batch=geo["tpb"], ctx_tiles=geo["ctx_tiles"], batch=geo["batch"])
    tile = lambda width, c=0: pl.BlockSpec((ROW_TILE, width), lambda i: (i, c))
    whole = lambda shape: pl.BlockSpec(shape, lambda i: (0,) * len(shape))
    in_specs = [tile(d_h), tile(d_h), tile(d_h, 4), tile(d_p), tile(d),
                pl.BlockSpec((1, N_ADA, d), lambda i: (row(i), 0, 0)),
                whole((1, d_h)), whole((groups, LANES, LANES)), whole((1, d_p)), whole((d_h + d_p, d)),
                whole((1, d))]
    args = [o_fwd, o_bwd, z, pdiff, h, mods, hgrn_norm_w.reshape(1, d_h), w_pool, pool_scale.reshape(1, d_p),
            w_out, norm_w.reshape(1, d)]
    out_specs = [tile(d)]
    out_shape = [jax.ShapeDtypeStruct((n, d), F32)]
    scratch = []
    if n_exp:
        in_specs.append(whole((d, 2 * LANES)))
        rw = jnp.pad(router_w.astype(F32), ((0, 0), (0, LANES - n_exp)))
        rw_hi = rw.astype(BF16)
        args.append(jnp.concatenate([rw_hi, (rw - rw_hi.astype(F32)).astype(BF16)], axis=1))
        out_specs += [tile(d // 2), tile(LANES), whole((8, LANES))]
        out_shape += [jax.ShapeDtypeStruct((n, d // 2), U32), jax.ShapeDtypeStruct((n, LANES), F32),
                      jax.ShapeDtypeStruct((8, LANES), F32)]
        scratch = [pltpu.VMEM((1, LANES), F32)]
    else:
        out_specs.append(tile(d))
        out_shape.append(jax.ShapeDtypeStruct((n, d), BF16))
    return pl.pallas_call(
        functools.partial(_mix_out_kernel, heads=d_h // HEAD_DIM, groups=groups, n_exp=n_exp),
        grid=(n // ROW_TILE,),
        in_specs=in_specs,
        out_specs=out_specs,
        out_shape=out_shape,
        scratch_shapes=scratch,
        compiler_params=_params("arbitrary"),
        name="mix_out",
    )(*args)


def _residual_out(h_ref, mod_ref, y, fnw_ref, out_ref):
    h2 = h_ref[...] + mod_ref[0, 5:6, :] * y
    out_ref[...] = h2 if fnw_ref is None else _rms(h2, fnw_ref[...])


def _ffn_kernel(*refs, final):
    x_ref, h_ref, mod_ref, wg_ref, wu_ref, wd_ref = refs[:6]
    fnw_ref = refs[6] if final else None
    out_ref = refs[-1]
    x = x_ref[...]
    a = (_silu(_dot(x, wg_ref[...])) * _dot(x, wu_ref[...])).astype(BF16)
    _residual_out(h_ref, mod_ref, _dot(a, wd_ref[...]), fnw_ref, out_ref)


def _tile_maps(geo, final):
    tpb, ctx_tiles, batch = geo["tpb"], geo["ctx_tiles"], geo["batch"]
    if final:
        lat = tpb - ctx_tiles
        return (batch, lat), (lambda b, j: b * tpb + ctx_tiles + j), (lambda b, j: b * lat + j), batch * lat * ROW_TILE
    return (batch, tpb), (lambda b, j: b * tpb + j), (lambda b, j: b * tpb + j), batch * tpb * ROW_TILE


def _ffn(v, h, mods, wg, wu, wd, final_norm_w, geo):
    n, d = h.shape
    d_ff = wg.shape[1]
    final = final_norm_w is not None
    grid, in_tile, out_tile, out_rows = _tile_maps(geo, final)
    whole = lambda shape: pl.BlockSpec(shape, lambda b, j: (0,) * len(shape))
    in_specs = [pl.BlockSpec((ROW_TILE, d), lambda b, j: (in_tile(b, j), 0)),
                pl.BlockSpec((ROW_TILE, d), lambda b, j: (in_tile(b, j), 0)),
                pl.BlockSpec((1, N_ADA, d), lambda b, j: (b, 0, 0)) if final else
                pl.BlockSpec((1, N_ADA, d), lambda b, j: (jnp.where(j < geo["ctx_tiles"], geo["batch"], b), 0, 0)),
                whole((d, d_ff)), whole((d, d_ff)), whole((d_ff, d))]
    args = [v, h, mods, wg, wu, wd]
    if final:
        in_specs.append(whole((1, d)))
        args.append(final_norm_w.reshape(1, d))
    return pl.pallas_call(
        functools.partial(_ffn_kernel, final=final),
        grid=grid,
        in_specs=in_specs,
        out_specs=pl.BlockSpec((ROW_TILE, d), lambda b, j: (out_tile(b, j), 0)),
        out_shape=jax.ShapeDtypeStruct((out_rows, d), F32),
        compiler_params=_params("arbitrary", "arbitrary"),
        name="ffn",
    )(*args)


def _dispatch_kernel(pos_ref, ends_ref, v_ref, xs_ref, zero_ref, sem, zero_sem):
    i = pl.program_id(0)
    rows = v_ref.shape[0]

    @pl.when(i == 0)
    def _():
        zero_ref[...] = jnp.zeros_like(zero_ref)
        tile = zero_ref.shape[0]
        n_exp = ends_ref.shape[0]

        def zero_copy(e):
            start = pl.multiple_of(ends_ref[e] - tile, tile)
            return pltpu.make_async_copy(zero_ref, xs_ref.at[pl.ds(start, tile), :], zero_sem)

        def nonempty(e):
            return ends_ref[e] > (ends_ref[e - 1] if e else 0)

        for e in range(n_exp):
            @pl.when(nonempty(e))
            def _(e=e):
                zero_copy(e).start()
        for e in range(n_exp):
            @pl.when(nonempty(e))
            def _(e=e):
                zero_copy(e).wait()

    def row_copy(r, k):
        return pltpu.make_async_copy(v_ref.at[pl.ds(r, 1), :], xs_ref.at[pl.ds(pos_ref[i, TOP_K * r + k], 1), :], sem)

    def issue(r, carry):
        for k in range(TOP_K):
            row_copy(r, k).start(priority=k)
        return carry

    for r in range(rows):
        issue(r, 0)
    for _ in range(TOP_K):
        pltpu.make_async_copy(v_ref, xs_ref.at[pl.ds(0, rows), :], sem).wait()


def _dispatch(v_packed, pos, ends, total_rows):
    n, w = v_packed.shape
    grid_spec = pltpu.PrefetchScalarGridSpec(
        num_scalar_prefetch=2,
        grid=(n // ROW_TILE,),
        in_specs=[pl.BlockSpec((ROW_TILE, w), lambda i, pos, ends: (i, 0))],
        out_specs=pl.BlockSpec(memory_space=pl.ANY),
        scratch_shapes=[pltpu.VMEM((EXPERT_TILE, w), U32), pltpu.SemaphoreType.DMA(()),
                        pltpu.SemaphoreType.DMA(())],
    )
    return pl.pallas_call(
        _dispatch_kernel,
        grid_spec=grid_spec,
        out_shape=jax.ShapeDtypeStruct((total_rows, w), U32),
        compiler_params=_params("arbitrary"),
        name="dispatch",
    )(pos, ends, v_packed)


def _experts_kernel(te_ref, na_ref, xs_ref, wg_ref, wu_ref, wd_ref, y_ref, acc_ref, xlo_ref, xhi_ref):
    t, f = pl.program_id(0), pl.program_id(1)
    last = pl.num_programs(1) - 1
    half = xlo_ref.shape[1]
    active = t < na_ref[0]

    @pl.when(active & (f == 0))
    def _():
        xlo_ref[...], xhi_ref[...] = _unpack_bf16_pairs(xs_ref[...])
        acc_ref[...] = jnp.zeros_like(acc_ref)

    @pl.when(active)
    def _():
        xlo, xhi = xlo_ref[...], xhi_ref[...]
        g = _dot(xlo, wg_ref[0, 0:half, :]) + _dot(xhi, wg_ref[0, half:, :])
        u = _dot(xlo, wu_ref[0, 0:half, :]) + _dot(xhi, wu_ref[0, half:, :])
        acc_ref[...] += _dot((_silu(g) * u).astype(BF16), wd_ref[0])

    @pl.when(active & (f == last))
    def _():
        y_ref[...] = acc_ref[...]

    @pl.when(jnp.logical_not(active) & (f == last))
    def _():
        y_ref[...] = jnp.zeros_like(y_ref)


def _experts(xs, tile_expert, n_active, wg, wu, wd):
    rows, half = xs.shape
    n_exp, d, d_ff = wg.shape
    tf = 512
    n_f = d_ff // tf
    n_tiles = rows // EXPERT_TILE

    def live(t, na):
        return jnp.minimum(t, na[0] - 1)

    def up(t, f, te, na):
        return (te[live(t, na)], 0, jnp.where(t < na[0], f, n_f - 1))

    def down(t, f, te, na):
        return (te[live(t, na)], jnp.where(t < na[0], f, n_f - 1), 0)

    grid_spec = pltpu.PrefetchScalarGridSpec(
        num_scalar_prefetch=2,
        grid=(n_tiles, n_f),
        in_specs=[pl.BlockSpec((EXPERT_TILE, half), lambda t, f, te, na: (live(t, na), 0)),
                  pl.BlockSpec((1, d, tf), up), pl.BlockSpec((1, d, tf), up), pl.BlockSpec((1, tf, d), down)],
        out_specs=pl.BlockSpec((EXPERT_TILE, d), lambda t, f, te, na: (t, 0)),
        scratch_shapes=[pltpu.VMEM((EXPERT_TILE, d), F32), pltpu.VMEM((EXPERT_TILE, half), BF16),
                        pltpu.VMEM((EXPERT_TILE, half), BF16)],
    )
    return pl.pallas_call(
        _experts_kernel,
        grid_spec=grid_spec,
        out_shape=jax.ShapeDtypeStruct((rows, d), F32),
        compiler_params=_params("arbitrary", "arbitrary"),
        name="experts",
    )(tile_expert, n_active, xs, wg, wu, wd)


def _combine_kernel(*refs, in_tile, final):
    pos_ref, h_ref, route_ref, mod_ref = refs[:4]
    fnw_ref = refs[4] if final else None
    y_ref, out_ref, ybuf_ref, sem = refs[-4:]
    i = in_tile(pl.program_id(0), pl.program_id(1))
    rows = h_ref.shape[0]

    def issue(r, carry):
        for k in range(TOP_K):
            pltpu.make_async_copy(y_ref.at[pl.ds(pos_ref[i, TOP_K * r + k], 1), :],
                                  ybuf_ref.at[k, pl.ds(r, 1), :], sem).start(priority=k)
        return carry

    for r in range(rows):
        issue(r, 0)
    for k in range(TOP_K):
        pltpu.make_async_copy(y_ref.at[pl.ds(0, rows), :], ybuf_ref.at[k], sem).wait()
    y = route_ref[:, 4:5] * ybuf_ref[0] + route_ref[:, 5:6] * ybuf_ref[1]
    _residual_out(h_ref, mod_ref, y, fnw_ref, out_ref)


def _combine(y_sorted, pos, route, h, mods, final_norm_w, geo):
    n, d = h.shape
    final = final_norm_w is not None
    grid, in_tile, out_tile, out_rows = _tile_maps(geo, final)
    in_specs = [pl.BlockSpec((ROW_TILE, d), lambda b, j, pos: (in_tile(b, j), 0)),
                pl.BlockSpec((ROW_TILE, LANES), lambda b, j, pos: (in_tile(b, j), 0)),
                pl.BlockSpec((1, N_ADA, d), lambda b, j, pos: (b, 0, 0)) if final else
                pl.BlockSpec((1, N_ADA, d),
                             lambda b, j, pos: (jnp.where(j < geo["ctx_tiles"], geo["batch"], b), 0, 0))]
    args = [h, route, mods]
    if final:
        in_specs.append(pl.BlockSpec((1, d), lambda b, j, pos: (0, 0)))
        args.append(final_norm_w.reshape(1, d))
    in_specs.append(pl.BlockSpec(memory_space=pl.ANY))
    args.append(y_sorted)
    grid_spec = pltpu.PrefetchScalarGridSpec(
        num_scalar_prefetch=1,
        grid=grid,
        in_specs=in_specs,
        out_specs=pl.BlockSpec((ROW_TILE, d), lambda b, j, pos: (out_tile(b, j), 0)),
        scratch_shapes=[pltpu.VMEM((TOP_K, ROW_TILE, d), F32), pltpu.SemaphoreType.DMA(())],
    )
    return pl.pallas_call(
        functools.partial(_combine_kernel, in_tile=in_tile, final=final),
        grid_spec=grid_spec,
        out_shape=jax.ShapeDtypeStruct((out_rows, d), F32),
        compiler_params=_params("arbitrary", "arbitrary"),
        name="combine",
    )(pos, *args)


def _routing_tables(route, counts, n_exp, n_tiles_max):
    experts = route[:, 0:TOP_K].astype(I32)
    slots = route[:, TOP_K:2 * TOP_K].astype(I32)
    cnt = counts[0, :n_exp].astype(I32)
    padded = (cnt + EXPERT_TILE - 1) // EXPERT_TILE * EXPERT_TILE
    ends = jnp.cumsum(padded)
    starts = ends - padded
    base = jnp.sum(jnp.where(experts[..., None] == jnp.arange(n_exp, dtype=I32), starts, 0), axis=-1)
    pos = (base + slots).reshape(-1, TOP_K * ROW_TILE)
    tile_start = jnp.arange(n_tiles_max, dtype=I32) * EXPERT_TILE
    tile_expert = jnp.minimum(jnp.sum(tile_start[:, None] >= ends[None, :], axis=-1), n_exp - 1).astype(I32)
    n_active = (ends[-1:] // EXPERT_TILE).astype(I32)
    return pos, ends.astype(I32), tile_expert, n_active


def _moe(v_packed, route, counts, h, mods, wg, wu, wd, final_norm_w, geo):
    n = h.shape[0]
    n_exp = wg.shape[0]
    n_tiles_max = -(-n * TOP_K // EXPERT_TILE) + n_exp
    pos, ends, tile_expert, n_active = _routing_tables(route, counts, n_exp, n_tiles_max)
    xs = _dispatch(v_packed, pos, ends, n_tiles_max * EXPERT_TILE)
    ys = _experts(xs, tile_expert, n_active, wg, wu, wd)
    return _combine(ys, pos, route, h, mods, final_norm_w, geo)


def kernel(x, c, ctx, c_ctx, w_ada, b_ada, norm_w, w_in, hgrn_lb_raw, hgrn_norm_w, w_pool, pool_scale, w_out,
           ffn_wg, ffn_wu, ffn_wd, router_w, moe_wg, moe_wu, moe_wd, final_norm_w):
    batch, seq_len, d = x.shape
    ctx_len = ctx.shape[1]
    depth = w_in.shape[0]
    seq = ctx_len + seq_len
    assert ctx_len % ROW_TILE == 0 and seq_len % ROW_TILE == 0 and seq_len % GRID_W == 0
    geo = dict(batch=batch, seq=seq, ctx_len=ctx_len, tpb=seq // ROW_TILE, ctx_tiles=ctx_len // ROW_TILE)

    lb_cum = jnp.cumsum(jax.nn.softmax(hgrn_lb_raw.astype(F32), axis=0), axis=0)
    lower_bounds = lb_cum - lb_cum[:1]

    cond_rows = -(-(batch + 1) // 8) * 8
    cond = jnp.concatenate([c, c_ctx[None, :], jnp.zeros((cond_rows - batch - 1, d), F32)], axis=0)
    mods_all = _modulation(cond, w_ada, b_ada).reshape(depth, cond_rows, N_ADA, d)

    h = jnp.concatenate([ctx, x], axis=1).reshape(batch * seq, d)
    for l in range(depth):
        mods = mods_all[l]
        final_w = final_norm_w if l == depth - 1 else None
        z = _mix_in(h, mods, norm_w[l, 0], w_in[l].astype(BF16), geo)
        o_fwd, o_bwd = _hgrn(z, lower_bounds[l], geo)
        pdiff = _pool(z, geo)
        dense = l % 2 == 0
        outs = _mix_out(o_fwd, o_bwd, z, pdiff, h, mods, hgrn_norm_w[l], w_pool[l].astype(BF16), pool_scale[l],
                        w_out[l].astype(BF16), norm_w[l, 1], None if dense else router_w[l // 2], geo)
        if dense:
            h1, v = outs
            h = _ffn(v, h1, mods, ffn_wg[l // 2].astype(BF16), ffn_wu[l // 2].astype(BF16),
                     ffn_wd[l // 2].astype(BF16), final_w, geo)
        else:
            h1, v_packed, route, counts = outs
            h = _moe(v_packed, route, counts, h1, mods, moe_wg[l // 2].astype(BF16), moe_wu[l // 2].astype(BF16),
                     moe_wd[l // 2].astype(BF16), final_w, geo)
    return h.reshape(batch, seq_len, d)
```

```python
import functools

import jax
import jax.numpy as jnp
from jax import lax
from jax.experimental import pallas as pl
from jax.experimental.pallas import tpu as pltpu

F32 = jnp.float32
BF16 = jnp.bfloat16
I32 = jnp.int32
U32 = jnp.uint32

HEAD_DIM = 128
GRID_W = 64
POOL_WINDOWS = (2, 4, 8, 16)
N_ADA = 6
TOP_K = 2
EPS = 1e-6

LANES = 128
ROW_TILE = 256
HGRN_CHUNK = 128
EXPERT_TILE = 1024
HGRN_FAST_LIMIT = 160.0
VMEM_LIMIT_BYTES = 48 * 1024 * 1024

_NT = (((1,), (1,)), ((), ()))


def _params(*semantics):
    return pltpu.CompilerParams(dimension_semantics=semantics, vmem_limit_bytes=VMEM_LIMIT_BYTES)


def _dot(a, b):
    return jnp.dot(a, b, preferred_element_type=F32)


def _silu(x):
    return x * jax.nn.sigmoid(x)


def _rms(x, w):
    return x * lax.rsqrt(jnp.mean(x * x, axis=-1, keepdims=True) + EPS) * w


def _mod_row(tile, tiles_per_batch, ctx_tiles, batch):
    return jnp.where(tile % tiles_per_batch < ctx_tiles, batch, tile // tiles_per_batch)


def _mod_kernel(c_ref, w_ref, b_ref, o_ref):
    o_ref[0] = jnp.dot(_silu(c_ref[...]), w_ref[0], preferred_element_type=F32,
                       precision=lax.Precision.HIGHEST) + b_ref[0]


def _modulation(cond, w_ada, b_ada):
    depth, d, n_out = w_ada.shape
    rows = cond.shape[0]
    tn = 1024
    return pl.pallas_call(
        _mod_kernel,
        grid=(depth, n_out // tn),
        in_specs=[
            pl.BlockSpec((rows, d), lambda l, j: (0, 0)),
            pl.BlockSpec((1, d, tn), lambda l, j: (l, 0, j)),
            pl.BlockSpec((1, 1, tn), lambda l, j: (l, 0, j)),
        ],
        out_specs=pl.BlockSpec((1, rows, tn), lambda l, j: (l, 0, j)),
        out_shape=jax.ShapeDtypeStruct((depth, rows, n_out), F32),
        compiler_params=_params("arbitrary", "arbitrary"),
        name="modulation",
    )(cond, w_ada, b_ada.reshape(depth, 1, n_out))


def _mix_in_kernel(h_ref, mod_ref, nw_ref, w_ref, z_ref):
    u = _rms(h_ref[...], nw_ref[...]) * (1 + mod_ref[0, 1:2, :]) + mod_ref[0, 0:1, :]
    z_ref[...] = _dot(u.astype(BF16), w_ref[...])


def _mix_in(h, mods, norm_w, w_in, geo):
    n, d = h.shape
    d_in = w_in.shape[1]
    row = functools.partial(_mod_row, tiles_per_batch=geo["tpb"], ctx_tiles=geo["ctx_tiles"], batch=geo["batch"])
    return pl.pallas_call(
        _mix_in_kernel,
        grid=(n // ROW_TILE,),
        in_specs=[
            pl.BlockSpec((ROW_TILE, d), lambda i: (i, 0)),
            pl.BlockSpec((1, N_ADA, d), lambda i: (row(i), 0, 0)),
            pl.BlockSpec((1, d), lambda i: (0, 0)),
            pl.BlockSpec((d, d_in), lambda i: (0, 0)),
        ],
        out_specs=pl.BlockSpec((ROW_TILE, d_in), lambda i: (i, 0)),
        out_shape=jax.ShapeDtypeStruct((n, d_in), F32),
        compiler_params=_params("arbitrary"),
        name="mix_in",
    )(h, mods, norm_w.reshape(1, d), w_in)


def _split3(x):
    hi = x.astype(BF16)
    r1 = x - hi.astype(F32)
    mid = r1.astype(BF16)
    lo = (r1 - mid.astype(F32)).astype(BF16)
    return hi, mid, lo


def _chunk_cumsum(lg_ref, cum_ref, *, c, reverse):
    n = lg_ref.shape[0]
    t_idx = lax.broadcasted_iota(I32, (n, n), 0)
    s_idx = lax.broadcasted_iota(I32, (n, n), 1)
    incl = (s_idx >= t_idx) if reverse else (s_idx <= t_idx)
    same_chunk = (t_idx // c) == (s_idx // c)
    tri = jnp.where(jnp.logical_and(incl, same_chunk), 1.0, 0.0).astype(BF16)
    hi, mid, lo = _split3(lg_ref[...])
    cum_ref[...] = _dot(tri, hi) + _dot(tri, mid) + _dot(tri, lo)


def _hgrn_chunk(q_ref, v_ref, cum_ref, k_ref, o_ref, st_ref, row0, *, c, reverse, fast, heads):
    rows = pl.ds(row0, c)
    t_idx = lax.broadcasted_iota(I32, (c, c), 0)
    s_idx = lax.broadcasted_iota(I32, (c, c), 1)
    incl = (s_idx >= t_idx) if reverse else (s_idx <= t_idx)
    b = cum_ref[rows, :]
    b_end = b[0:1, :] if reverse else b[c - 1:c, :]
    q = q_ref[rows, :] * (HEAD_DIM ** -0.5)
    k = k_ref[rows, :]
    v = v_ref[rows, :]

    if fast:
        r = 0.5 * b_end
        er = jnp.exp(r)
        qt = (q * jnp.exp(b - r)).astype(BF16)
        kt = (k * jnp.exp(r - b)).astype(BF16)
        for h in range(heads):
            hs = slice(h * HEAD_DIM, (h + 1) * HEAD_DIM)
            scores = lax.dot_general(qt[:, hs], kt[:, hs], _NT, preferred_element_type=F32)
            p = jnp.where(incl, scores, 0.0).astype(BF16)
            st = st_ref[h] * er[:, hs]
            o = _dot(p, v[:, hs].astype(BF16)) + lax.dot_general(
                qt[:, hs], st.astype(BF16), _NT, preferred_element_type=F32)
            o_ref[rows, hs] = o
            st_ref[h] = (st + _dot(v[:, hs].T.astype(BF16), kt[:, hs])) * er[:, hs]
    else:
        qd = (q * jnp.exp(b)).astype(BF16)
        ke = (k * jnp.exp(b_end - b)).astype(BF16)
        decay = jnp.exp(b_end)
        for h in range(heads):
            hs = slice(h * HEAD_DIM, (h + 1) * HEAD_DIM)
            st = st_ref[h]
            o_ref[rows, hs] = lax.dot_general(qd[:, hs], st.astype(BF16), _NT, preferred_element_type=F32)
            st_ref[h] = st * decay[:, hs] + _dot(v[:, hs].T.astype(BF16), ke[:, hs])
        t_col = lax.broadcasted_iota(I32, (c, HEAD_DIM), 0)

        def key_step(s, carry):
            b_s = cum_ref[pl.ds(row0 + s, 1), :]
            k_s = k_ref[pl.ds(row0 + s, 1), :]
            v_s = v_ref[pl.ds(row0 + s, 1), :]
            prod = q * jnp.exp(jnp.minimum(b - b_s, 0.0)) * k_s
            visible = (t_col <= s) if reverse else (t_col >= s)
            for h in range(heads):
                hs = slice(h * HEAD_DIM, (h + 1) * HEAD_DIM)
                w = jnp.sum(prod[:, hs], axis=-1, keepdims=True)
                o_ref[rows, hs] += jnp.where(visible, w, 0.0) * v_s[:, hs]
            return carry

        lax.fori_loop(0, c, key_step, 0)


def _hgrn_kernel(qf_ref, vf_ref, ff_ref, qb_ref, vb_ref, fb_ref, lb_ref, of_ref, ob_ref,
                 stf_ref, stb_ref, lgf_ref, lgb_ref, kf_ref, kb_ref, cumf_ref, cumb_ref, *, heads):
    block = qf_ref.shape[0]
    half = HGRN_CHUNK // 2
    n_half = block // half

    @pl.when(pl.program_id(1) == 0)
    def _():
        stf_ref[...] = jnp.zeros_like(stf_ref)
        stb_ref[...] = jnp.zeros_like(stb_ref)

    worst_full = jnp.float32(0.0)
    worst_half = jnp.float32(0.0)
    for d, (f_ref, lg_ref, k_ref) in enumerate(((ff_ref, lgf_ref, kf_ref), (fb_ref, lgb_ref, kb_ref))):
        lb = lb_ref[d:d + 1, :]
        f = lb + (1 - lb) * jax.nn.sigmoid(f_ref[...])
        lg = jnp.log(f)
        lg_ref[...] = lg
        k_ref[...] = 1 - f
        half_decay = -jnp.sum(lg.reshape(n_half, half, lg.shape[-1]), axis=1)
        worst_half = jnp.maximum(worst_half, jnp.max(half_decay))
        for i in range(0, n_half, 2):
            worst_full = jnp.maximum(worst_full, jnp.max(half_decay[i:i + 1] + half_decay[i + 1:i + 2]))
    full_ok = worst_full < HGRN_FAST_LIMIT
    half_ok = jnp.logical_and(jnp.logical_not(full_ok), worst_half < HGRN_FAST_LIMIT)
    neither = jnp.logical_and(jnp.logical_not(full_ok), jnp.logical_not(worst_half < HGRN_FAST_LIMIT))

    def run(c, fast):
        n_chunks = block // c
        _chunk_cumsum(lgf_ref, cumf_ref, c=c, reverse=False)
        _chunk_cumsum(lgb_ref, cumb_ref, c=c, reverse=True)

        def body(ci, carry):
            fwd_row = ci * c
            bwd_row = (n_chunks - 1 - ci) * c
            if not isinstance(ci, int):
                fwd_row, bwd_row = pl.multiple_of(fwd_row, c), pl.multiple_of(bwd_row, c)
            _hgrn_chunk(qf_ref, vf_ref, cumf_ref, kf_ref, of_ref, stf_ref, fwd_row,
                        c=c, reverse=False, fast=fast, heads=heads)
            _hgrn_chunk(qb_ref, vb_ref, cumb_ref, kb_ref, ob_ref, stb_ref, bwd_row,
                        c=c, reverse=True, fast=fast, heads=heads)
            return carry

        if fast:
            for ci in range(n_chunks):
                body(ci, 0)
        else:
            lax.fori_loop(0, n_chunks, body, 0)

    @pl.when(full_ok)
    def _():
        run(HGRN_CHUNK, True)

    @pl.when(half_ok)
    def _():
        run(half, True)

    @pl.when(neither)
    def _():
        run(HGRN_CHUNK, False)


def _hgrn(z, lower_bounds, geo):
    n = z.shape[0]
    d_h = lower_bounds.shape[-1]
    heads = d_h // HEAD_DIM
    nblk, ctx_blocks, batch = geo["tpb"], geo["ctx_tiles"], geo["batch"]

    def fwd_block(b, j):
        return b * nblk + j

    def bwd_block(b, j):
        return b * nblk + jnp.where(j < ctx_blocks, ctx_blocks - 1 - j, nblk - 1 - (j - ctx_blocks))

    def col(block_fn, c):
        return pl.BlockSpec((ROW_TILE, d_h), lambda b, j: (block_fn(b, j), c))

    state = pltpu.VMEM((heads, HEAD_DIM, HEAD_DIM), F32)
    rows = pltpu.VMEM((ROW_TILE, d_h), F32)
    return pl.pallas_call(
        functools.partial(_hgrn_kernel, heads=heads),
        grid=(batch, nblk),
        in_specs=[col(fwd_block, 0), col(fwd_block, 1), col(fwd_block, 2),
                  col(bwd_block, 0), col(bwd_block, 1), col(bwd_block, 3),
                  pl.BlockSpec((2, d_h), lambda b, j: (0, 0))],
        out_specs=[col(fwd_block, 0), col(bwd_block, 0)],
        out_shape=[jax.ShapeDtypeStruct((n, d_h), F32)] * 2,
        scratch_shapes=[state, state, rows, rows, rows, rows, rows, rows],
        compiler_params=_params("arbitrary", "arbitrary"),
        name="hgrn",
    )(z, z, z, z, z, z, lower_bounds)


def _window_sum(x, k):
    n = x.shape[0]
    t = lax.broadcasted_iota(I32, x.shape, 0)

    def ahead(a, d):
        return jnp.where(t + d < n, pltpu.roll(a, n - d, axis=0), 0.0)

    def behind(a, d):
        return jnp.where(t >= d, pltpu.roll(a, d, axis=0), 0.0)

    fwd = x
    bwd = behind(x, 1)
    w = 1
    while 2 * w <= k // 2:
        fwd = fwd + ahead(fwd, w)
        bwd = bwd + behind(bwd, w)
        w *= 2
    return fwd + bwd


def _window_count(shape, n, k, offset=0):
    t = lax.broadcasted_iota(I32, shape, 0) + offset
    lo, hi = k // 2, k - 1 - k // 2
    return (jnp.minimum(t + hi + 1, n) - jnp.maximum(t - lo, 0)).astype(F32)


def _pool_group(x_ref, o_ref, cs_ref, k, ctx_len, grid_rows):
    lo, hi = k // 2, k - 1 - k // 2
    pad = POOL_WINDOWS[-1] // 2

    x = x_ref[0:ctx_len, :]
    mean = _window_sum(x, k) / _window_count(x.shape, ctx_len, k)
    o_ref[0:ctx_len, :] = (mean - x).astype(o_ref.dtype)

    zeros = jnp.zeros((pad * GRID_W, LANES), F32)
    cs_ref[0:pad * GRID_W, :] = zeros
    cs_ref[(pad + grid_rows) * GRID_W:(2 * pad + grid_rows) * GRID_W, :] = zeros

    def col_pass(r, carry):
        src = pl.multiple_of(ctx_len + r * GRID_W, GRID_W)
        dst = pl.multiple_of((pad + r) * GRID_W, GRID_W)
        cs_ref[pl.ds(dst, GRID_W), :] = _window_sum(x_ref[pl.ds(src, GRID_W), :], k)
        return carry

    lax.fori_loop(0, grid_rows, col_pass, 0)
    n_col = _window_count((GRID_W, LANES), GRID_W, k)

    def row_pass(r, carry):
        acc = jnp.zeros((GRID_W, LANES), F32)
        for dr in range(-lo, hi + 1):
            acc = acc + cs_ref[pl.ds(pl.multiple_of((pad + r + dr) * GRID_W, GRID_W), GRID_W), :]
        n_row = (jnp.minimum(r + hi + 1, grid_rows) - jnp.maximum(r - lo, 0)).astype(F32)
        src = pl.multiple_of(ctx_len + r * GRID_W, GRID_W)
        o_ref[pl.ds(src, GRID_W), :] = (acc / (n_row * n_col) - x_ref[pl.ds(src, GRID_W), :]).astype(o_ref.dtype)
        return carry

    lax.fori_loop(0, grid_rows, row_pass, 0)


def _pool_kernel(x_ref, o_ref, cs_ref, *, ctx_len, grid_rows):
    group = pl.program_id(1)
    for gi, k in enumerate(POOL_WINDOWS):
        @pl.when(group == gi)
        def _(k=k):
            _pool_group(x_ref, o_ref, cs_ref, k, ctx_len, grid_rows)


def _pool(z, geo):
    n, d_in = z.shape
    groups = len(POOL_WINDOWS)
    first = d_in // LANES - groups
    seq, ctx_len = geo["seq"], geo["ctx_len"]
    grid_rows = (seq - ctx_len) // GRID_W
    pad = POOL_WINDOWS[-1] // 2
    return pl.pallas_call(
        functools.partial(_pool_kernel, ctx_len=ctx_len, grid_rows=grid_rows),
        grid=(geo["batch"], groups),
        in_specs=[pl.BlockSpec((seq, LANES), lambda b, g: (b, first + g))],
        out_specs=pl.BlockSpec((seq, LANES), lambda b, g: (b, g)),
        out_shape=jax.ShapeDtypeStruct((n, groups * LANES), BF16),
        scratch_shapes=[pltpu.VMEM(((grid_rows + 2 * pad) * GRID_W, LANES), F32)],
        compiler_params=_params("arbitrary", "arbitrary"),
        name="pool",
    )(z)


def _pack_bf16_pairs(x):
    w = x.shape[1] // 2
    lo = lax.bitcast_convert_type(x[:, :w].astype(BF16).astype(F32), U32)
    hi = lax.bitcast_convert_type(x[:, w:].astype(BF16).astype(F32), U32)
    return (lo >> 16) | (hi & jnp.uint32(0xFFFF0000))


def _unpack_bf16_pairs(w):
    lo = lax.bitcast_convert_type(w << 16, F32).astype(BF16)
    hi = lax.bitcast_convert_type(w & jnp.uint32(0xFFFF0000), F32).astype(BF16)
    return lo, hi


ROUTE_POS_LANE = 6


def _route(v, rw_ref, carry_ref, n_exp, cap):
    rows = v.shape[0]
    v_hi = v.astype(BF16)
    v_lo = (v - v_hi.astype(F32)).astype(BF16)
    hi_prod = _dot(v_hi, rw_ref[...])
    logits = hi_prod[:, :LANES] + hi_prod[:, LANES:] + _dot(v_lo, rw_ref[:, :LANES])
    lane = lax.broadcasted_iota(I32, (rows, LANES), 1).astype(F32)
    neg = jnp.float32(-jnp.inf)
    logits = jnp.where(lane < n_exp, logits, neg)
    m1 = jnp.max(logits, axis=-1, keepdims=True)
    i1 = jnp.min(jnp.where(logits == m1, lane, float(LANES)), axis=-1, keepdims=True)
    rest = jnp.where(lane == i1, neg, logits)
    m2 = jnp.max(rest, axis=-1, keepdims=True)
    i2 = jnp.min(jnp.where(rest == m2, lane, float(LANES)), axis=-1, keepdims=True)
    e = jnp.exp(m2 - m1)
    w1 = 1.0 / (1.0 + e)
    w2 = e / (1.0 + e)
    chosen = jnp.where((lane == i1) | (lane == i2), 1.0, 0.0)
    t_idx = lax.broadcasted_iota(I32, (rows, rows), 0)
    s_idx = lax.broadcasted_iota(I32, (rows, rows), 1)
    before = jnp.where(s_idx < t_idx, 1.0, 0.0).astype(BF16)
    slots = _dot(before, chosen.astype(BF16)) + carry_ref[...]
    r1 = jnp.sum(jnp.where(lane == i1, slots, 0.0), axis=-1, keepdims=True)
    r2 = jnp.sum(jnp.where(lane == i2, slots, 0.0), axis=-1, keepdims=True)
    carry_ref[...] += jnp.sum(chosen, axis=0, keepdims=True)
    out = jnp.zeros((rows, LANES), F32)
    for idx, val in enumerate((i1, i2, r1, r2, w1, w2, i1 * cap + r1, i2 * cap + r2)):
        out = jnp.where(lane == idx, val, out)
    return out


def _mixer_tail(of_ref, ob_ref, g_ref, pd_ref, h_ref, mod_ref, hnw_ref, wp_ref, ps_ref, wo_ref, nw_ref, *,
                heads, groups):
    d_h = heads * HEAD_DIM
    o = of_ref[...] + ob_ref[...]
    normed = []
    for hd in range(heads):
        oh = o[:, hd * HEAD_DIM:(hd + 1) * HEAD_DIM]
        normed.append(oh * lax.rsqrt(jnp.mean(oh * oh, axis=-1, keepdims=True) + EPS))
    a = jnp.concatenate(normed, axis=-1) * hnw_ref[...] * _silu(g_ref[...])
    y = _dot(a.astype(BF16), wo_ref[0:d_h, :])
    for gi in range(groups):
        gs = slice(gi * LANES, (gi + 1) * LANES)
        p = _dot(pd_ref[:, gs], wp_ref[gi]) * ps_ref[:, gs]
        y = y + _dot(p.astype(BF16), wo_ref[d_h + gi * LANES:d_h + (gi + 1) * LANES, :])
    h1 = h_ref[...] + mod_ref[0, 2:3, :] * y
    v = _rms(h1, nw_ref[...]) * (1 + mod_ref[0, 4:5, :]) + mod_ref[0, 3:4, :]
    return h1, v


def _mix_out_kernel(*refs, heads, groups):
    h1_ref, v_ref = refs[11:]
    h1, v = _mixer_tail(*refs[:11], heads=heads, groups=groups)
    h1_ref[...] = h1
    v_ref[...] = v.astype(BF16)


def _mix_out_route_kernel(*refs, heads, groups, n_exp, cap):
    rw_ref, h1_ref, route_ref, cnt_ref, xs_ref = refs[11:16]
    carry_ref, vp0, vp1, pv0, pv1, ps0, ps1, pos_sem, row_sem = refs[16:]
    vp, pos_vmem, pos_smem = (vp0, vp1), (pv0, pv1), (ps0, ps1)
    rows = vp0.shape[0]
    i = pl.program_id(0)
    n_tiles = pl.num_programs(0) - 1

    def pos_copy(slot):
        return pltpu.make_async_copy(pos_vmem[slot], pos_smem[slot], pos_sem)

    def scatter(slot):
        pos_copy(slot).wait()
        for r in range(rows):
            for k in range(TOP_K):
                pltpu.make_async_copy(vp[slot].at[pl.ds(r, 1), :],
                                      xs_ref.at[pl.ds(pos_smem[slot][ROUTE_POS_LANE + k, r], 1), :],
                                      row_sem).start(priority=k)

    def scatter_wait(slot):
        for _ in range(TOP_K):
            pltpu.make_async_copy(vp[slot], xs_ref.at[pl.ds(0, rows), :], row_sem).wait()

    def compute(slot):
        h1, v = _mixer_tail(*refs[:11], heads=heads, groups=groups)
        h1_ref[...] = h1
        route = _route(v, rw_ref, carry_ref, n_exp, cap)
        route_ref[...] = route
        cnt_ref[...] = jnp.broadcast_to(carry_ref[...], cnt_ref.shape)
        vp[slot][...] = _pack_bf16_pairs(v)
        pos_vmem[slot][...] = route.T[0:pos_vmem[slot].shape[0], :].astype(I32)
        pos_copy(slot).start()

    @pl.when(i == 0)
    def _():
        carry_ref[...] = jnp.zeros_like(carry_ref)
        compute(0)

    for slot in (0, 1):
        @pl.when((i > 0) & (i < n_tiles) & (i % 2 == slot))
        def _(slot=slot):
            scatter(1 - slot)
            compute(slot)
            scatter_wait(1 - slot)

        @pl.when((i == n_tiles) & (i % 2 == slot))
        def _(slot=slot):
            scatter(1 - slot)
            scatter_wait(1 - slot)


def _mix_out(o_fwd, o_bwd, z, pdiff, h, mods, hgrn_norm_w, w_pool, pool_scale, w_out, norm_w, router_w, geo):
    n, d = h.shape
    d_h = o_fwd.shape[1]
    groups = w_pool.shape[0]
    d_p = groups * LANES
    n_tiles = n // ROW_TILE
    routed = router_w is not None
    clamp = (lambda i: jnp.minimum(i, n_tiles - 1)) if routed else (lambda i: i)
    row = functools.partial(_mod_row, tiles_per_batch=geo["tpb"], ctx_tiles=geo["ctx_tiles"], batch=geo["batch"])
    tile = lambda width, c=0: pl.BlockSpec((ROW_TILE, width), lambda i: (clamp(i), c))
    whole = lambda shape: pl.BlockSpec(shape, lambda i: (0,) * len(shape))
    in_specs = [tile(d_h), tile(d_h), tile(d_h, 4), tile(d_p), tile(d),
                pl.BlockSpec((1, N_ADA, d), lambda i: (row(clamp(i)), 0, 0)),
                whole((1, d_h)), whole((groups, LANES, LANES)), whole((1, d_p)), whole((d_h + d_p, d)),
                whole((1, d))]
    args = [o_fwd, o_bwd, z, pdiff, h, mods, hgrn_norm_w.reshape(1, d_h), w_pool, pool_scale.reshape(1, d_p),
            w_out, norm_w.reshape(1, d)]
    heads = d_h // HEAD_DIM
    if not routed:
        return pl.pallas_call(
            functools.partial(_mix_out_kernel, heads=heads, groups=groups),
            grid=(n_tiles,),
            in_specs=in_specs,
            out_specs=[tile(d), tile(d)],
            out_shape=[jax.ShapeDtypeStruct((n, d), F32), jax.ShapeDtypeStruct((n, d), BF16)],
            compiler_params=_params("arbitrary"),
            name="mix_out",
        )(*args)

    n_exp = router_w.shape[1]
    cap = _expert_capacity(n)
    rw = jnp.pad(router_w.astype(F32), ((0, 0), (0, LANES - n_exp)))
    rw_hi = rw.astype(BF16)
    args.append(jnp.concatenate([rw_hi, (rw - rw_hi.astype(F32)).astype(BF16)], axis=1))
    in_specs.append(whole((d, 2 * LANES)))
    packed = pltpu.VMEM((ROW_TILE, d // 2), U32)
    return pl.pallas_call(
        functools.partial(_mix_out_route_kernel, heads=heads, groups=groups, n_exp=n_exp, cap=cap),
        grid=(n_tiles + 1,),
        in_specs=in_specs,
        out_specs=[tile(d), tile(LANES), whole((8, LANES)), pl.BlockSpec(memory_space=pl.ANY)],
        out_shape=[jax.ShapeDtypeStruct((n, d), F32), jax.ShapeDtypeStruct((n, LANES), F32),
                   jax.ShapeDtypeStruct((8, LANES), F32), jax.ShapeDtypeStruct((n_exp * cap, d // 2), U32)],
        scratch_shapes=[pltpu.VMEM((1, LANES), F32), packed, packed,
                        pltpu.VMEM((8, ROW_TILE), I32), pltpu.VMEM((8, ROW_TILE), I32),
                        pltpu.SMEM((8, ROW_TILE), I32), pltpu.SMEM((8, ROW_TILE), I32),
                        pltpu.SemaphoreType.DMA(()), pltpu.SemaphoreType.DMA(())],
        compiler_params=_params("arbitrary"),
        name="mix_out_route",
    )(*args)


def _expert_capacity(n_tokens):
    return -(-n_tokens // EXPERT_TILE) * EXPERT_TILE


def _residual_out(h_ref, mod_ref, y, fnw_ref, out_ref):
    h2 = h_ref[...] + mod_ref[0, 5:6, :] * y
    out_ref[...] = h2 if fnw_ref is None else _rms(h2, fnw_ref[...])


def _ffn_kernel(*refs, final):
    x_ref, h_ref, mod_ref, wg_ref, wu_ref, wd_ref = refs[:6]
    fnw_ref = refs[6] if final else None
    out_ref = refs[-1]
    x = x_ref[...]
    a = (_silu(_dot(x, wg_ref[...])) * _dot(x, wu_ref[...])).astype(BF16)
    _residual_out(h_ref, mod_ref, _dot(a, wd_ref[...]), fnw_ref, out_ref)


def _tile_maps(geo, final):
    tpb, ctx_tiles, batch = geo["tpb"], geo["ctx_tiles"], geo["batch"]
    if final:
        lat = tpb - ctx_tiles
        return (batch, lat), (lambda b, j: b * tpb + ctx_tiles + j), (lambda b, j: b * lat + j), batch * lat * ROW_TILE
    return (batch, tpb), (lambda b, j: b * tpb + j), (lambda b, j: b * tpb + j), batch * tpb * ROW_TILE


def _ffn(v, h, mods, wg, wu, wd, final_norm_w, geo):
    n, d = h.shape
    d_ff = wg.shape[1]
    final = final_norm_w is not None
    grid, in_tile, out_tile, out_rows = _tile_maps(geo, final)
    whole = lambda shape: pl.BlockSpec(shape, lambda b, j: (0,) * len(shape))
    in_specs = [pl.BlockSpec((ROW_TILE, d), lambda b, j: (in_tile(b, j), 0)),
                pl.BlockSpec((ROW_TILE, d), lambda b, j: (in_tile(b, j), 0)),
                pl.BlockSpec((1, N_ADA, d), lambda b, j: (b, 0, 0)) if final else
                pl.BlockSpec((1, N_ADA, d), lambda b, j: (jnp.where(j < geo["ctx_tiles"], geo["batch"], b), 0, 0)),
                whole((d, d_ff)), whole((d, d_ff)), whole((d_ff, d))]
    args = [v, h, mods, wg, wu, wd]
    if final:
        in_specs.append(whole((1, d)))
        args.append(final_norm_w.reshape(1, d))
    return pl.pallas_call(
        functools.partial(_ffn_kernel, final=final),
        grid=grid,
        in_specs=in_specs,
        out_specs=pl.BlockSpec((ROW_TILE, d), lambda b, j: (out_tile(b, j), 0)),
        out_shape=jax.ShapeDtypeStruct((out_rows, d), F32),
        compiler_params=_params("arbitrary", "arbitrary"),
        name="ffn",
    )(*args)


def _zero_padding_kernel(cnt_ref, xs_in_ref, xs_ref, zero_ref, sem, *, cap):
    del xs_in_ref
    zero_ref[...] = jnp.zeros_like(zero_ref)

    def row_copy(e, r):
        return pltpu.make_async_copy(zero_ref.at[pl.ds(0, 1), :], xs_ref.at[pl.ds(e * cap + r, 1), :], sem)

    for e in range(cnt_ref.shape[0]):
        used = cnt_ref[e]
        tile_end = (used + EXPERT_TILE - 1) // EXPERT_TILE * EXPERT_TILE
        lax.fori_loop(used, tile_end, lambda r, c, e=e: (row_copy(e, r).start(), c)[1], 0)
        lax.fori_loop(used, tile_end, lambda r, c, e=e: (row_copy(e, r).wait(), c)[1], 0)


def _zero_padding(xs, cnt, cap):
    grid_spec = pltpu.PrefetchScalarGridSpec(
        num_scalar_prefetch=1,
        grid=(1,),
        in_specs=[pl.BlockSpec(memory_space=pl.ANY)],
        out_specs=pl.BlockSpec(memory_space=pl.ANY),
        scratch_shapes=[pltpu.VMEM((8, xs.shape[1]), U32), pltpu.SemaphoreType.DMA(())],
    )
    return pl.pallas_call(
        functools.partial(_zero_padding_kernel, cap=cap),
        grid_spec=grid_spec,
        out_shape=jax.ShapeDtypeStruct(xs.shape, xs.dtype),
        input_output_aliases={1: 0},
        compiler_params=_params("arbitrary"),
        name="zero_padding",
    )(cnt, xs)


def _experts_kernel(te_ref, tb_ref, na_ref, xs_ref, wg_ref, wu_ref, wd_ref, y_ref, acc_ref, xlo_ref, xhi_ref):
    t, f = pl.program_id(0), pl.program_id(1)
    last = pl.num_programs(1) - 1
    half = xlo_ref.shape[1]
    active = t < na_ref[0]

    @pl.when(active & (f == 0))
    def _():
        xlo_ref[...], xhi_ref[...] = _unpack_bf16_pairs(xs_ref[...])
        acc_ref[...] = jnp.zeros_like(acc_ref)

    @pl.when(active)
    def _():
        xlo, xhi = xlo_ref[...], xhi_ref[...]
        g = _dot(xlo, wg_ref[0, 0:half, :]) + _dot(xhi, wg_ref[0, half:, :])
        u = _dot(xlo, wu_ref[0, 0:half, :]) + _dot(xhi, wu_ref[0, half:, :])
        acc_ref[...] += _dot((_silu(g) * u).astype(BF16), wd_ref[0])

    @pl.when(active & (f == last))
    def _():
        y_ref[...] = acc_ref[...]

    @pl.when(jnp.logical_not(active) & (f == last))
    def _():
        y_ref[...] = jnp.zeros_like(y_ref)


def _experts(xs, tile_expert, tile_block, n_active, n_tiles, wg, wu, wd):
    rows, half = xs.shape
    _, d, d_ff = wg.shape
    tf = 512
    n_f = d_ff // tf
    spare = rows // EXPERT_TILE

    def live(t, na):
        return jnp.minimum(t, na[0] - 1)

    def up(t, f, te, tb, na):
        return (te[live(t, na)], 0, jnp.where(t < na[0], f, n_f - 1))

    def down(t, f, te, tb, na):
        return (te[live(t, na)], jnp.where(t < na[0], f, n_f - 1), 0)

    grid_spec = pltpu.PrefetchScalarGridSpec(
        num_scalar_prefetch=3,
        grid=(n_tiles, n_f),
        in_specs=[pl.BlockSpec((EXPERT_TILE, half), lambda t, f, te, tb, na: (tb[live(t, na)], 0)),
                  pl.BlockSpec((1, d, tf), up), pl.BlockSpec((1, d, tf), up), pl.BlockSpec((1, tf, d), down)],
        out_specs=pl.BlockSpec((EXPERT_TILE, d),
                               lambda t, f, te, tb, na: (jnp.where(t < na[0], tb[t], spare), 0)),
        scratch_shapes=[pltpu.VMEM((EXPERT_TILE, d), F32), pltpu.VMEM((EXPERT_TILE, half), BF16),
                        pltpu.VMEM((EXPERT_TILE, half), BF16)],
    )
    return pl.pallas_call(
        _experts_kernel,
        grid_spec=grid_spec,
        out_shape=jax.ShapeDtypeStruct((rows + EXPERT_TILE, d), F32),
        compiler_params=_params("arbitrary", "arbitrary"),
        name="experts",
    )(tile_expert, tile_block, n_active, xs, wg, wu, wd)


def _combine_kernel(*refs, in_tile, final):
    pos_ref, h_ref, route_ref, mod_ref = refs[:4]
    fnw_ref = refs[4] if final else None
    y_ref, out_ref, ybuf_ref, sem = refs[-4:]
    i = in_tile(pl.program_id(0), pl.program_id(1))
    rows = h_ref.shape[0]

    def issue(r, carry):
        for k in range(TOP_K):
            pltpu.make_async_copy(y_ref.at[pl.ds(pos_ref[i, TOP_K * r + k], 1), :],
                                  ybuf_ref.at[k, pl.ds(r, 1), :], sem).start(priority=k)
        return carry

    for r in range(rows):
        issue(r, 0)
    for k in range(TOP_K):
        pltpu.make_async_copy(y_ref.at[pl.ds(0, rows), :], ybuf_ref.at[k], sem).wait()
    y = route_ref[:, 4:5] * ybuf_ref[0] + route_ref[:, 5:6] * ybuf_ref[1]
    _residual_out(h_ref, mod_ref, y, fnw_ref, out_ref)


def _combine(y_sorted, pos, route, h, mods, final_norm_w, geo):
    n, d = h.shape
    final = final_norm_w is not None
    grid, in_tile, out_tile, out_rows = _tile_maps(geo, final)
    in_specs = [pl.BlockSpec((ROW_TILE, d), lambda b, j, pos: (in_tile(b, j), 0)),
                pl.BlockSpec((ROW_TILE, LANES), lambda b, j, pos: (in_tile(b, j), 0)),
                pl.BlockSpec((1, N_ADA, d), lambda b, j, pos: (b, 0, 0)) if final else
                pl.BlockSpec((1, N_ADA, d),
                             lambda b, j, pos: (jnp.where(j < geo["ctx_tiles"], geo["batch"], b), 0, 0))]
    args = [h, route, mods]
    if final:
        in_specs.append(pl.BlockSpec((1, d), lambda b, j, pos: (0, 0)))
        args.append(final_norm_w.reshape(1, d))
    in_specs.append(pl.BlockSpec(memory_space=pl.ANY))
    args.append(y_sorted)
    grid_spec = pltpu.PrefetchScalarGridSpec(
        num_scalar_prefetch=1,
        grid=grid,
        in_specs=in_specs,
        out_specs=pl.BlockSpec((ROW_TILE, d), lambda b, j, pos: (out_tile(b, j), 0)),
        scratch_shapes=[pltpu.VMEM((TOP_K, ROW_TILE, d), F32), pltpu.SemaphoreType.DMA(())],
    )
    return pl.pallas_call(
        functools.partial(_combine_kernel, in_tile=in_tile, final=final),
        grid_spec=grid_spec,
        out_shape=jax.ShapeDtypeStruct((out_rows, d), F32),
        compiler_params=_params("arbitrary", "arbitrary"),
        name="combine",
    )(pos, *args)


def _tile_schedule(cnt, n_tiles, cap):
    n_exp = cnt.shape[0]
    tiles = (cnt + EXPERT_TILE - 1) // EXPERT_TILE
    ends = jnp.cumsum(tiles)
    t = jnp.arange(n_tiles, dtype=I32)
    expert = jnp.minimum(jnp.sum(t[:, None] >= ends[None, :], axis=-1), n_exp - 1).astype(I32)
    first = jnp.sum(jnp.where(expert[:, None] == jnp.arange(n_exp, dtype=I32), ends - tiles, 0), axis=-1)
    block = expert * (cap // EXPERT_TILE) + (t - first)
    return expert, block.astype(I32), ends[-1:].astype(I32)


def _moe(xs, route, counts, h, mods, wg, wu, wd, first_expert, n_exp, final_norm_w, geo):
    n = h.shape[0]
    cap = _expert_capacity(n)
    n_tiles = -(-n * TOP_K // EXPERT_TILE) + n_exp
    cnt = counts[0, :n_exp].astype(I32)
    tile_expert, tile_block, n_active = _tile_schedule(cnt, n_tiles, cap)
    xs = _zero_padding(xs, cnt, cap)
    ys = _experts(xs, tile_expert + first_expert, tile_block, n_active, n_tiles, wg, wu, wd)
    pos = route[:, ROUTE_POS_LANE:ROUTE_POS_LANE + TOP_K].astype(I32).reshape(-1, TOP_K * ROW_TILE)
    return _combine(ys, pos, route, h, mods, final_norm_w, geo)


def kernel(x, c, ctx, c_ctx, w_ada, b_ada, norm_w, w_in, hgrn_lb_raw, hgrn_norm_w, w_pool, pool_scale, w_out,
           ffn_wg, ffn_wu, ffn_wd, router_w, moe_wg, moe_wu, moe_wd, final_norm_w):
    batch, seq_len, d = x.shape
    ctx_len = ctx.shape[1]
    depth = w_in.shape[0]
    seq = ctx_len + seq_len
    assert ctx_len % ROW_TILE == 0 and seq_len % ROW_TILE == 0 and seq_len % GRID_W == 0
    geo = dict(batch=batch, seq=seq, ctx_len=ctx_len, tpb=seq // ROW_TILE, ctx_tiles=ctx_len // ROW_TILE)

    lb_cum = jnp.cumsum(jax.nn.softmax(hgrn_lb_raw.astype(F32), axis=0), axis=0)
    lower_bounds = lb_cum - lb_cum[:1]

    cond_rows = -(-(batch + 1) // 8) * 8
    cond = jnp.concatenate([c, c_ctx[None, :], jnp.zeros((cond_rows - batch - 1, d), F32)], axis=0)
    mods_all = _modulation(cond, w_ada, b_ada).reshape(depth, cond_rows, N_ADA, d)

    n_exp = moe_wg.shape[1]
    moe_w = [w.astype(BF16).reshape((-1,) + w.shape[2:]) for w in (moe_wg, moe_wu, moe_wd)]

    h = jnp.concatenate([ctx, x], axis=1).reshape(batch * seq, d)
    for l in range(depth):
        mods = mods_all[l]
        final_w = final_norm_w if l == depth - 1 else None
        z = _mix_in(h, mods, norm_w[l, 0], w_in[l].astype(BF16), geo)
        o_fwd, o_bwd = _hgrn(z, lower_bounds[l], geo)
        pdiff = _pool(z, geo)
        dense = l % 2 == 0
        outs = _mix_out(o_fwd, o_bwd, z, pdiff, h, mods, hgrn_norm_w[l], w_pool[l].astype(BF16), pool_scale[l],
                        w_out[l].astype(BF16), norm_w[l, 1], None if dense else router_w[l // 2], geo)
        if dense:
            h1, v = outs
            h = _ffn(v, h1, mods, ffn_wg[l // 2].astype(BF16), ffn_wu[l // 2].astype(BF16),
                     ffn_wd[l // 2].astype(BF16), final_w, geo)
        else:
            h1, route, counts, xs = outs
            h = _moe(xs, route, counts, h1, mods, *moe_w, (l // 2) * n_exp, n_exp, final_w, geo)
    return h.reshape(batch, seq_len, d)
```

```python
import functools

import jax
import jax.numpy as jnp
from jax import lax
from jax.experimental import pallas as pl
from jax.experimental.pallas import tpu as pltpu

F32 = jnp.float32
BF16 = jnp.bfloat16
I32 = jnp.int32
U32 = jnp.uint32

HEAD_DIM = 128
GRID_W = 64
POOL_WINDOWS = (2, 4, 8, 16)
N_ADA = 6
TOP_K = 2
EPS = 1e-6

LANES = 128
ROW_TILE = 256
HGRN_CHUNK = 128
HGRN_MEMBERS = 2
EXPERT_TILE = 1024
HGRN_FAST_LIMIT = 160.0
VMEM_LIMIT_BYTES = 48 * 1024 * 1024

_NT = (((1,), (1,)), ((), ()))


def _params(*semantics):
    return pltpu.CompilerParams(dimension_semantics=semantics, vmem_limit_bytes=VMEM_LIMIT_BYTES)


def _dot(a, b):
    return jnp.dot(a, b, preferred_element_type=F32)


def _silu(x):
    return x * jax.nn.sigmoid(x)


def _rms(x, w):
    return x * lax.rsqrt(jnp.mean(x * x, axis=-1, keepdims=True) + EPS) * w


def _mod_row(tile, tiles_per_batch, ctx_tiles, batch):
    return jnp.where(tile % tiles_per_batch < ctx_tiles, batch, tile // tiles_per_batch)


def _mod_kernel(c_ref, w_ref, b_ref, o_ref):
    o_ref[0] = jnp.dot(_silu(c_ref[...]), w_ref[0], preferred_element_type=F32,
                       precision=lax.Precision.HIGHEST) + b_ref[0]


def _modulation(cond, w_ada, b_ada):
    depth, d, n_out = w_ada.shape
    rows = cond.shape[0]
    tn = 1024
    return pl.pallas_call(
        _mod_kernel,
        grid=(depth, n_out // tn),
        in_specs=[
            pl.BlockSpec((rows, d), lambda l, j: (0, 0)),
            pl.BlockSpec((1, d, tn), lambda l, j: (l, 0, j)),
            pl.BlockSpec((1, 1, tn), lambda l, j: (l, 0, j)),
        ],
        out_specs=pl.BlockSpec((1, rows, tn), lambda l, j: (l, 0, j)),
        out_shape=jax.ShapeDtypeStruct((depth, rows, n_out), F32),
        compiler_params=_params("arbitrary", "arbitrary"),
        name="modulation",
    )(cond, w_ada, b_ada.reshape(depth, 1, n_out))


def _stream_specs(h, geo):
    if not isinstance(h, tuple):
        return [pl.BlockSpec((ROW_TILE, h.shape[1]), lambda i: (i, 0))], [h]
    tpb, ct, batch = geo["tpb"], geo["ctx_tiles"], geo["batch"]
    lat = tpb - ct
    d = h[0].shape[1]

    def ctx_block(i):
        b, j = i // tpb, i % tpb
        return (jnp.where(j < ct, b * ct + j, jnp.minimum(b + 1, batch - 1) * ct), 0)

    def lat_block(i):
        b, j = i // tpb, i % tpb
        return (jnp.where(j < ct, b * lat, b * lat + j - ct), 0)

    return [pl.BlockSpec((ROW_TILE, d), ctx_block), pl.BlockSpec((ROW_TILE, d), lat_block)], list(h)


def _stream_tile(refs, geo):
    if len(refs) == 1:
        return refs[0][...]
    is_ctx = pl.program_id(0) % geo["tpb"] < geo["ctx_tiles"]
    return jnp.where(is_ctx, refs[0][...], refs[1][...])


def _mix_in_kernel(*refs, geo):
    mod_ref, nw_ref, w_ref, z_ref = refs[-4:]
    u = _rms(_stream_tile(refs[:-4], geo), nw_ref[...]) * (1 + mod_ref[0, 1:2, :]) + mod_ref[0, 0:1, :]
    z_ref[...] = _dot(u.astype(BF16), w_ref[...])


def _mix_in(h, mods, norm_w, w_in, geo):
    d, d_in = w_in.shape
    n = geo["batch"] * geo["seq"]
    row = functools.partial(_mod_row, tiles_per_batch=geo["tpb"], ctx_tiles=geo["ctx_tiles"], batch=geo["batch"])
    stream_specs, stream = _stream_specs(h, geo)
    return pl.pallas_call(
        functools.partial(_mix_in_kernel, geo=geo),
        grid=(n // ROW_TILE,),
        in_specs=stream_specs + [
            pl.BlockSpec((1, N_ADA, d), lambda i: (row(i), 0, 0)),
            pl.BlockSpec((1, d), lambda i: (0, 0)),
            pl.BlockSpec((d, d_in), lambda i: (0, 0)),
        ],
        out_specs=pl.BlockSpec((ROW_TILE, d_in), lambda i: (i, 0)),
        out_shape=jax.ShapeDtypeStruct((n, d_in), F32),
        compiler_params=_params("arbitrary"),
        name="mix_in",
    )(*stream, mods, norm_w.reshape(1, d), w_in)


def _split3(x):
    hi = x.astype(BF16)
    r1 = x - hi.astype(F32)
    mid = r1.astype(BF16)
    lo = (r1 - mid.astype(F32)).astype(BF16)
    return hi, mid, lo


def _chunk_cumsum(lg_ref, cum_ref, *, c, reverse):
    n = lg_ref.shape[0]
    t_idx = lax.broadcasted_iota(I32, (n, n), 0)
    s_idx = lax.broadcasted_iota(I32, (n, n), 1)
    incl = (s_idx >= t_idx) if reverse else (s_idx <= t_idx)
    same_chunk = (t_idx // c) == (s_idx // c)
    tri = jnp.where(jnp.logical_and(incl, same_chunk), 1.0, 0.0).astype(BF16)
    hi, mid, lo = _split3(lg_ref[...])
    cum_ref[...] = _dot(tri, hi) + _dot(tri, mid) + _dot(tri, lo)


def _hgrn_chunk(q_ref, v_ref, cum_ref, k_ref, o_ref, st_ref, row0, *, c, reverse, fast, heads):
    rows = pl.ds(row0, c)
    t_idx = lax.broadcasted_iota(I32, (c, c), 0)
    s_idx = lax.broadcasted_iota(I32, (c, c), 1)
    incl = (s_idx >= t_idx) if reverse else (s_idx <= t_idx)
    b = cum_ref[rows, :]
    b_end = b[0:1, :] if reverse else b[c - 1:c, :]
    q = q_ref[rows, :] * (HEAD_DIM ** -0.5)
    k = k_ref[rows, :]
    v = v_ref[rows, :]

    if fast:
        r = 0.5 * b_end
        er = jnp.exp(r)
        qt = (q * jnp.exp(b - r)).astype(BF16)
        kt = (k * jnp.exp(r - b)).astype(BF16)
        for h in range(heads):
            hs = slice(h * HEAD_DIM, (h + 1) * HEAD_DIM)
            scores = lax.dot_general(qt[:, hs], kt[:, hs], _NT, preferred_element_type=F32)
            p = jnp.where(incl, scores, 0.0).astype(BF16)
            st = st_ref[h] * er[:, hs]
            o = _dot(p, v[:, hs].astype(BF16)) + lax.dot_general(
                qt[:, hs], st.astype(BF16), _NT, preferred_element_type=F32)
            o_ref[rows, hs] = o
            st_ref[h] = (st + _dot(v[:, hs].T.astype(BF16), kt[:, hs])) * er[:, hs]
    else:
        qd = (q * jnp.exp(b)).astype(BF16)
        ke = (k * jnp.exp(b_end - b)).astype(BF16)
        decay = jnp.exp(b_end)
        for h in range(heads):
            hs = slice(h * HEAD_DIM, (h + 1) * HEAD_DIM)
            st = st_ref[h]
            o_ref[rows, hs] = lax.dot_general(qd[:, hs], st.astype(BF16), _NT, preferred_element_type=F32)
            st_ref[h] = st * decay[:, hs] + _dot(v[:, hs].T.astype(BF16), ke[:, hs])
        t_col = lax.broadcasted_iota(I32, (c, HEAD_DIM), 0)

        def key_step(s, carry):
            b_s = cum_ref[pl.ds(row0 + s, 1), :]
            k_s = k_ref[pl.ds(row0 + s, 1), :]
            v_s = v_ref[pl.ds(row0 + s, 1), :]
            prod = q * jnp.exp(jnp.minimum(b - b_s, 0.0)) * k_s
            visible = (t_col <= s) if reverse else (t_col >= s)
            for h in range(heads):
                hs = slice(h * HEAD_DIM, (h + 1) * HEAD_DIM)
                w = jnp.sum(prod[:, hs], axis=-1, keepdims=True)
                o_ref[rows, hs] += jnp.where(visible, w, 0.0) * v_s[:, hs]
            return carry

        lax.fori_loop(0, c, key_step, 0)


def _hgrn_kernel(qf_ref, vf_ref, ff_ref, qb_ref, vb_ref, fb_ref, lb_ref, of_ref, ob_ref,
                 stf_ref, stb_ref, lgf_ref, lgb_ref, kf_ref, kb_ref, cumf_ref, cumb_ref, *, heads):
    members, block = qf_ref.shape[0], qf_ref.shape[1]
    half = HGRN_CHUNK // 2
    n_half = block // half

    @pl.when(pl.program_id(1) == 0)
    def _():
        stf_ref[...] = jnp.zeros_like(stf_ref)
        stb_ref[...] = jnp.zeros_like(stb_ref)

    worst_full = jnp.float32(0.0)
    worst_half = jnp.float32(0.0)
    for m in range(members):
        for d, (f_ref, lg_ref, k_ref) in enumerate(((ff_ref, lgf_ref, kf_ref), (fb_ref, lgb_ref, kb_ref))):
            lb = lb_ref[d:d + 1, :]
            f = lb + (1 - lb) * jax.nn.sigmoid(f_ref[m])
            lg = jnp.log(f)
            lg_ref[m] = lg
            k_ref[m] = 1 - f
            half_decay = -jnp.sum(lg.reshape(n_half, half, lg.shape[-1]), axis=1)
            worst_half = jnp.maximum(worst_half, jnp.max(half_decay))
            for i in range(0, n_half, 2):
                worst_full = jnp.maximum(worst_full, jnp.max(half_decay[i:i + 1] + half_decay[i + 1:i + 2]))
    full_ok = worst_full < HGRN_FAST_LIMIT
    half_ok = jnp.logical_and(jnp.logical_not(full_ok), worst_half < HGRN_FAST_LIMIT)
    neither = jnp.logical_and(jnp.logical_not(full_ok), jnp.logical_not(worst_half < HGRN_FAST_LIMIT))

    def run(c, fast):
        n_chunks = block // c
        for m in range(members):
            _chunk_cumsum(lgf_ref.at[m], cumf_ref.at[m], c=c, reverse=False)
            _chunk_cumsum(lgb_ref.at[m], cumb_ref.at[m], c=c, reverse=True)

        def body(ci, carry):
            fwd_row = ci * c
            bwd_row = (n_chunks - 1 - ci) * c
            if not isinstance(ci, int):
                fwd_row, bwd_row = pl.multiple_of(fwd_row, c), pl.multiple_of(bwd_row, c)
            for m in range(members):
                _hgrn_chunk(qf_ref.at[m], vf_ref.at[m], cumf_ref.at[m], kf_ref.at[m], of_ref.at[m],
                            stf_ref.at[m], fwd_row, c=c, reverse=False, fast=fast, heads=heads)
                _hgrn_chunk(qb_ref.at[m], vb_ref.at[m], cumb_ref.at[m], kb_ref.at[m], ob_ref.at[m],
                            stb_ref.at[m], bwd_row, c=c, reverse=True, fast=fast, heads=heads)
            return carry

        if fast:
            for ci in range(n_chunks):
                body(ci, 0)
        else:
            lax.fori_loop(0, n_chunks, body, 0)

    @pl.when(full_ok)
    def _():
        run(HGRN_CHUNK, True)

    @pl.when(half_ok)
    def _():
        run(half, True)

    @pl.when(neither)
    def _():
        run(HGRN_CHUNK, False)


def _hgrn(z, lower_bounds, geo):
    n = z.shape[0]
    d_h = lower_bounds.shape[-1]
    heads = d_h // HEAD_DIM
    nblk, ctx_blocks, batch, seq = geo["tpb"], geo["ctx_tiles"], geo["batch"], geo["seq"]
    members = HGRN_MEMBERS if batch % HGRN_MEMBERS == 0 else 1
    z3 = z.reshape(batch, seq, z.shape[1])

    def fwd_block(j):
        return j

    def bwd_block(j):
        return jnp.where(j < ctx_blocks, ctx_blocks - 1 - j, nblk - 1 - (j - ctx_blocks))

    def col(block_fn, c):
        return pl.BlockSpec((members, ROW_TILE, d_h), lambda b, j: (b, block_fn(j), c))

    state = pltpu.VMEM((members, heads, HEAD_DIM, HEAD_DIM), F32)
    rows = pltpu.VMEM((members, ROW_TILE, d_h), F32)
    o_fwd, o_bwd = pl.pallas_call(
        functools.partial(_hgrn_kernel, heads=heads),
        grid=(batch // members, nblk),
        in_specs=[col(fwd_block, 0), col(fwd_block, 1), col(fwd_block, 2),
                  col(bwd_block, 0), col(bwd_block, 1), col(bwd_block, 3),
                  pl.BlockSpec((2, d_h), lambda b, j: (0, 0))],
        out_specs=[col(fwd_block, 0), col(bwd_block, 0)],
        out_shape=[jax.ShapeDtypeStruct((batch, seq, d_h), F32)] * 2,
        scratch_shapes=[state, state, rows, rows, rows, rows, rows, rows],
        compiler_params=_params("arbitrary", "arbitrary"),
        name="hgrn",
    )(z3, z3, z3, z3, z3, z3, lower_bounds)
    return o_fwd.reshape(n, d_h), o_bwd.reshape(n, d_h)


def _window_sum(x, k):
    n = x.shape[0]
    t = lax.broadcasted_iota(I32, x.shape, 0)

    def ahead(a, d):
        return jnp.where(t + d < n, pltpu.roll(a, n - d, axis=0), 0.0)

    def behind(a, d):
        return jnp.where(t >= d, pltpu.roll(a, d, axis=0), 0.0)

    fwd = x
    bwd = behind(x, 1)
    w = 1
    while 2 * w <= k // 2:
        fwd = fwd + ahead(fwd, w)
        bwd = bwd + behind(bwd, w)
        w *= 2
    return fwd + bwd


def _window_count(shape, n, k, offset=0):
    t = lax.broadcasted_iota(I32, shape, 0) + offset
    lo, hi = k // 2, k - 1 - k // 2
    return (jnp.minimum(t + hi + 1, n) - jnp.maximum(t - lo, 0)).astype(F32)


def _pool_group(x_ref, o_ref, cs_ref, k, ctx_len, grid_rows):
    lo, hi = k // 2, k - 1 - k // 2
    pad = POOL_WINDOWS[-1] // 2

    x = x_ref[0:ctx_len, :]
    mean = _window_sum(x, k) / _window_count(x.shape, ctx_len, k)
    o_ref[0:ctx_len, :] = (mean - x).astype(o_ref.dtype)

    zeros = jnp.zeros((pad * GRID_W, LANES), F32)
    cs_ref[0:pad * GRID_W, :] = zeros
    cs_ref[(pad + grid_rows) * GRID_W:(2 * pad + grid_rows) * GRID_W, :] = zeros

    def col_pass(r, carry):
        src = pl.multiple_of(ctx_len + r * GRID_W, GRID_W)
        dst = pl.multiple_of((pad + r) * GRID_W, GRID_W)
        cs_ref[pl.ds(dst, GRID_W), :] = _window_sum(x_ref[pl.ds(src, GRID_W), :], k)
        return carry

    lax.fori_loop(0, grid_rows, col_pass, 0)
    n_col = _window_count((GRID_W, LANES), GRID_W, k)

    def row_pass(r, carry):
        acc = jnp.zeros((GRID_W, LANES), F32)
        for dr in range(-lo, hi + 1):
            acc = acc + cs_ref[pl.ds(pl.multiple_of((pad + r + dr) * GRID_W, GRID_W), GRID_W), :]
        n_row = (jnp.minimum(r + hi + 1, grid_rows) - jnp.maximum(r - lo, 0)).astype(F32)
        src = pl.multiple_of(ctx_len + r * GRID_W, GRID_W)
        o_ref[pl.ds(src, GRID_W), :] = (acc / (n_row * n_col) - x_ref[pl.ds(src, GRID_W), :]).astype(o_ref.dtype)
        return carry

    lax.fori_loop(0, grid_rows, row_pass, 0)


def _pool_kernel(x_ref, o_ref, cs_ref, *, ctx_len, grid_rows):
    group = pl.program_id(1)
    for gi, k in enumerate(POOL_WINDOWS):
        @pl.when(group == gi)
        def _(k=k):
            _pool_group(x_ref, o_ref, cs_ref, k, ctx_len, grid_rows)


def _pool(z, geo):
    n, d_in = z.shape
    groups = len(POOL_WINDOWS)
    first = d_in // LANES - groups
    seq, ctx_len = geo["seq"], geo["ctx_len"]
    grid_rows = (seq - ctx_len) // GRID_W
    pad = POOL_WINDOWS[-1] // 2
    return pl.pallas_call(
        functools.partial(_pool_kernel, ctx_len=ctx_len, grid_rows=grid_rows),
        grid=(geo["batch"], groups),
        in_specs=[pl.BlockSpec((seq, LANES), lambda b, g: (b, first + g))],
        out_specs=pl.BlockSpec((seq, LANES), lambda b, g: (b, g)),
        out_shape=jax.ShapeDtypeStruct((n, groups * LANES), BF16),
        scratch_shapes=[pltpu.VMEM(((grid_rows + 2 * pad) * GRID_W, LANES), F32)],
        compiler_params=_params("arbitrary", "arbitrary"),
        name="pool",
    )(z)


def _pack_bf16_pairs(x):
    w = x.shape[1] // 2
    lo = lax.bitcast_convert_type(x[:, :w].astype(BF16).astype(F32), U32)
    hi = lax.bitcast_convert_type(x[:, w:].astype(BF16).astype(F32), U32)
    return (lo >> 16) | (hi & jnp.uint32(0xFFFF0000))


def _unpack_bf16_pairs(w):
    lo = lax.bitcast_convert_type(w << 16, F32).astype(BF16)
    hi = lax.bitcast_convert_type(w & jnp.uint32(0xFFFF0000), F32).astype(BF16)
    return lo, hi


ROUTE_POS_LANE = 6


def _route(v, rw_ref, carry_ref, n_exp, cap):
    rows = v.shape[0]
    v_hi = v.astype(BF16)
    v_lo = (v - v_hi.astype(F32)).astype(BF16)
    hi_prod = _dot(v_hi, rw_ref[...])
    logits = hi_prod[:, :LANES] + hi_prod[:, LANES:] + _dot(v_lo, rw_ref[:, :LANES])
    lane = lax.broadcasted_iota(I32, (rows, LANES), 1).astype(F32)
    neg = jnp.float32(-jnp.inf)
    logits = jnp.where(lane < n_exp, logits, neg)
    m1 = jnp.max(logits, axis=-1, keepdims=True)
    i1 = jnp.min(jnp.where(logits == m1, lane, float(LANES)), axis=-1, keepdims=True)
    rest = jnp.where(lane == i1, neg, logits)
    m2 = jnp.max(rest, axis=-1, keepdims=True)
    i2 = jnp.min(jnp.where(rest == m2, lane, float(LANES)), axis=-1, keepdims=True)
    e = jnp.exp(m2 - m1)
    w1 = 1.0 / (1.0 + e)
    w2 = e / (1.0 + e)
    chosen = jnp.where((lane == i1) | (lane == i2), 1.0, 0.0)
    t_idx = lax.broadcasted_iota(I32, (rows, rows), 0)
    s_idx = lax.broadcasted_iota(I32, (rows, rows), 1)
    before = jnp.where(s_idx < t_idx, 1.0, 0.0).astype(BF16)
    slots = _dot(before, chosen.astype(BF16)) + carry_ref[...]
    r1 = jnp.sum(jnp.where(lane == i1, slots, 0.0), axis=-1, keepdims=True)
    r2 = jnp.sum(jnp.where(lane == i2, slots, 0.0), axis=-1, keepdims=True)
    carry_ref[...] += jnp.sum(chosen, axis=0, keepdims=True)
    out = jnp.zeros((rows, LANES), F32)
    for idx, val in enumerate((i1, i2, r1, r2, w1, w2, i1 * cap + r1, i2 * cap + r2)):
        out = jnp.where(lane == idx, val, out)
    return out


def _mixer_tail(of_ref, ob_ref, g_ref, pd_ref, h, mod_ref, hnw_ref, wp_ref, ps_ref, wo_ref, nw_ref, *,
                heads, groups):
    d_h = heads * HEAD_DIM
    o = of_ref[...] + ob_ref[...]
    normed = []
    for hd in range(heads):
        oh = o[:, hd * HEAD_DIM:(hd + 1) * HEAD_DIM]
        normed.append(oh * lax.rsqrt(jnp.mean(oh * oh, axis=-1, keepdims=True) + EPS))
    a = jnp.concatenate(normed, axis=-1) * hnw_ref[...] * _silu(g_ref[...])
    y = _dot(a.astype(BF16), wo_ref[0:d_h, :])
    for gi in range(groups):
        gs = slice(gi * LANES, (gi + 1) * LANES)
        p = _dot(pd_ref[:, gs], wp_ref[gi]) * ps_ref[:, gs]
        y = y + _dot(p.astype(BF16), wo_ref[d_h + gi * LANES:d_h + (gi + 1) * LANES, :])
    h1 = h + mod_ref[0, 2:3, :] * y
    v = _rms(h1, nw_ref[...]) * (1 + mod_ref[0, 4:5, :]) + mod_ref[0, 3:4, :]
    return h1, v


def _mix_out_kernel(*refs, heads, groups, geo):
    h1_ref, v_ref = refs[-2:]
    stream_refs = refs[4:-8]
    h1, v = _mixer_tail(*refs[:4], _stream_tile(stream_refs, geo), *refs[-8:-2], heads=heads, groups=groups)
    h1_ref[...] = h1
    v_ref[...] = v.astype(BF16)


def _mix_out_route_kernel(*refs, heads, groups, n_exp, cap):
    rw_ref, h1_ref, route_ref, cnt_ref, xs_ref = refs[11:16]
    carry_ref, vp0, vp1, pv0, pv1, ps0, ps1, pos_sem, row_sem = refs[16:]
    vp, pos_vmem, pos_smem = (vp0, vp1), (pv0, pv1), (ps0, ps1)
    rows = vp0.shape[0]
    i = pl.program_id(0)
    n_tiles = pl.num_programs(0) - 1

    def pos_copy(slot):
        return pltpu.make_async_copy(pos_vmem[slot], pos_smem[slot], pos_sem)

    def scatter(slot):
        pos_copy(slot).wait()
        for r in range(rows):
            for k in range(TOP_K):
                pltpu.make_async_copy(vp[slot].at[pl.ds(r, 1), :],
                                      xs_ref.at[pl.ds(pos_smem[slot][ROUTE_POS_LANE + k, r], 1), :],
                                      row_sem).start(priority=k)

    def scatter_wait(slot):
        for _ in range(TOP_K):
            pltpu.make_async_copy(vp[slot], xs_ref.at[pl.ds(0, rows), :], row_sem).wait()

    def compute(slot):
        h1, v = _mixer_tail(*refs[:4], refs[4][...], *refs[5:11], heads=heads, groups=groups)
        h1_ref[...] = h1
        route =_route(v, rw_ref, carry_ref, n_exp, cap)
        route_ref[...] = route
        cnt_ref[...] = jnp.broadcast_to(carry_ref[...], cnt_ref.shape)
        vp[slot][...] = _pack_bf16_pairs(v)
        pos_vmem[slot][...] = route.T[0:pos_vmem[slot].shape[0], :].astype(I32)
        pos_copy(slot).start()

    @pl.when(i == 0)
    def _():
        carry_ref[...] = jnp.zeros_like(carry_ref)
        compute(0)

    for slot in (0, 1):
        @pl.when((i > 0) & (i < n_tiles) & (i % 2 == slot))
        def _(slot=slot):
            scatter(1 - slot)
            compute(slot)
            scatter_wait(1 - slot)

        @pl.when((i == n_tiles) & (i % 2 == slot))
        def _(slot=slot):
            scatter(1 - slot)
            scatter_wait(1 - slot)


def _mix_out(o_fwd, o_bwd, z, pdiff, h, mods, hgrn_norm_w, w_pool, pool_scale, w_out, norm_w, router_w, geo):
    n, d = o_fwd.shape[0], w_out.shape[1]
    d_h = o_fwd.shape[1]
    groups = w_pool.shape[0]
    d_p = groups * LANES
    n_tiles = n // ROW_TILE
    routed = router_w is not None
    clamp = (lambda i: jnp.minimum(i, n_tiles - 1)) if routed else (lambda i: i)
    row = functools.partial(_mod_row, tiles_per_batch=geo["tpb"], ctx_tiles=geo["ctx_tiles"], batch=geo["batch"])
    tile = lambda width, c=0: pl.BlockSpec((ROW_TILE, width), lambda i: (clamp(i), c))
    whole = lambda shape: pl.BlockSpec(shape, lambda i: (0,) * len(shape))
    stream_specs, stream = ([tile(d)], [h]) if routed else _stream_specs(h, geo)
    in_specs = [tile(d_h), tile(d_h), tile(d_h, 4), tile(d_p), *stream_specs,
                pl.BlockSpec((1, N_ADA, d), lambda i: (row(clamp(i)), 0, 0)),
                whole((1, d_h)), whole((groups, LANES, LANES)), whole((1, d_p)), whole((d_h + d_p, d)),
                whole((1, d))]
    args = [o_fwd, o_bwd, z, pdiff, *stream, mods, hgrn_norm_w.reshape(1, d_h), w_pool,
            pool_scale.reshape(1, d_p), w_out, norm_w.reshape(1, d)]
    heads = d_h // HEAD_DIM
    if not routed:
        return pl.pallas_call(
            functools.partial(_mix_out_kernel, heads=heads, groups=groups, geo=geo),
            grid=(n_tiles,),
            in_specs=in_specs,
            out_specs=[tile(d), tile(d)],
            out_shape=[jax.ShapeDtypeStruct((n, d), F32), jax.ShapeDtypeStruct((n, d), BF16)],
            compiler_params=_params("arbitrary"),
            name="mix_out",
        )(*args)

    n_exp = router_w.shape[1]
    cap = _expert_capacity(n)
    rw = jnp.pad(router_w.astype(F32), ((0, 0), (0, LANES - n_exp)))
    rw_hi = rw.astype(BF16)
    args.append(jnp.concatenate([rw_hi, (rw - rw_hi.astype(F32)).astype(BF16)], axis=1))
    in_specs.append(whole((d, 2 * LANES)))
    packed = pltpu.VMEM((ROW_TILE, d // 2), U32)
    return pl.pallas_call(
        functools.partial(_mix_out_route_kernel, heads=heads, groups=groups, n_exp=n_exp, cap=cap),
        grid=(n_tiles + 1,),
        in_specs=in_specs,
        out_specs=[tile(d), tile(LANES), whole((8, LANES)), pl.BlockSpec(memory_space=pl.ANY)],
        out_shape=[jax.ShapeDtypeStruct((n, d), F32), jax.ShapeDtypeStruct((n, LANES), F32),
                   jax.ShapeDtypeStruct((8, LANES), F32), jax.ShapeDtypeStruct((n_exp * cap, d // 2), U32)],
        scratch_shapes=[pltpu.VMEM((1, LANES), F32), packed, packed,
                        pltpu.VMEM((8, ROW_TILE), I32), pltpu.VMEM((8, ROW_TILE), I32),
                        pltpu.SMEM((8, ROW_TILE), I32), pltpu.SMEM((8, ROW_TILE), I32),
                        pltpu.SemaphoreType.DMA(()), pltpu.SemaphoreType.DMA(())],
        compiler_params=_params("arbitrary"),
        name="mix_out_route",
    )(*args)


def _expert_capacity(n_tokens):
    return -(-n_tokens // EXPERT_TILE) * EXPERT_TILE


def _residual_out(h_ref, mod_ref, y, fnw_ref, out_ref):
    h2 = h_ref[...] + mod_ref[0, 5:6, :] * y
    out_ref[...] = h2 if fnw_ref is None else _rms(h2, fnw_ref[...])


def _ffn_kernel(*refs, final):
    x_ref, h_ref, mod_ref, wg_ref, wu_ref, wd_ref = refs[:6]
    fnw_ref = refs[6] if final else None
    out_ref = refs[-1]
    x = x_ref[...]
    a = (_silu(_dot(x, wg_ref[...])) * _dot(x, wu_ref[...])).astype(BF16)
    _residual_out(h_ref, mod_ref, _dot(a, wd_ref[...]), fnw_ref, out_ref)


def _tile_maps(geo, final):
    tpb, ctx_tiles, batch = geo["tpb"], geo["ctx_tiles"], geo["batch"]
    if final:
        lat = tpb - ctx_tiles
        return (batch, lat), (lambda b, j: b * tpb + ctx_tiles + j), (lambda b, j: b * lat + j), batch * lat * ROW_TILE
    return (batch, tpb), (lambda b, j: b * tpb + j), (lambda b, j: b * tpb + j), batch * tpb * ROW_TILE


def _ffn(v, h, mods, wg, wu, wd, final_norm_w, geo):
    n, d = h.shape
    d_ff = wg.shape[1]
    final = final_norm_w is not None
    grid, in_tile, out_tile, out_rows = _tile_maps(geo, final)
    whole = lambda shape: pl.BlockSpec(shape, lambda b, j: (0,) * len(shape))
    in_specs = [pl.BlockSpec((ROW_TILE, d), lambda b, j: (in_tile(b, j), 0)),
                pl.BlockSpec((ROW_TILE, d), lambda b, j: (in_tile(b, j), 0)),
                pl.BlockSpec((1, N_ADA, d), lambda b, j: (b, 0, 0)) if final else
                pl.BlockSpec((1, N_ADA, d), lambda b, j: (jnp.where(j < geo["ctx_tiles"], geo["batch"], b), 0, 0)),
                whole((d, d_ff)), whole((d, d_ff)), whole((d_ff, d))]
    args = [v, h, mods, wg, wu, wd]
    if final:
        in_specs.append(whole((1, d)))
        args.append(final_norm_w.reshape(1, d))
    return pl.pallas_call(
        functools.partial(_ffn_kernel, final=final),
        grid=grid,
        in_specs=in_specs,
        out_specs=pl.BlockSpec((ROW_TILE, d), lambda b, j: (out_tile(b, j), 0)),
        out_shape=jax.ShapeDtypeStruct((out_rows, d), F32),
        compiler_params=_params("arbitrary", "arbitrary"),
        name="ffn",
    )(*args)


def _zero_padding_kernel(cnt_ref, xs_in_ref, xs_ref, zero_ref, sem, *, cap):
    del xs_in_ref
    zero_ref[...] = jnp.zeros_like(zero_ref)

    def row_copy(e, r):
        return pltpu.make_async_copy(zero_ref.at[pl.ds(0, 1), :], xs_ref.at[pl.ds(e * cap + r, 1), :], sem)

    for e in range(cnt_ref.shape[0]):
        used = cnt_ref[e]
        tile_end = (used + EXPERT_TILE - 1) // EXPERT_TILE * EXPERT_TILE
        lax.fori_loop(used, tile_end, lambda r, c, e=e: (row_copy(e, r).start(), c)[1], 0)
        lax.fori_loop(used, tile_end, lambda r, c, e=e: (row_copy(e, r).wait(), c)[1], 0)


def _zero_padding(xs, cnt, cap):
    grid_spec = pltpu.PrefetchScalarGridSpec(
        num_scalar_prefetch=1,
        grid=(1,),
        in_specs=[pl.BlockSpec(memory_space=pl.ANY)],
        out_specs=pl.BlockSpec(memory_space=pl.ANY),
        scratch_shapes=[pltpu.VMEM((8, xs.shape[1]), U32), pltpu.SemaphoreType.DMA(())],
    )
    return pl.pallas_call(
        functools.partial(_zero_padding_kernel, cap=cap),
        grid_spec=grid_spec,
        out_shape=jax.ShapeDtypeStruct(xs.shape, xs.dtype),
        input_output_aliases={1: 0},
        compiler_params=_params("arbitrary"),
        name="zero_padding",
    )(cnt, xs)


def _experts_kernel(te_ref, tb_ref, na_ref, xs_ref, wg_ref, wu_ref, wd_ref, y_ref, acc_ref, xlo_ref, xhi_ref):
    t, f = pl.program_id(0), pl.program_id(1)
    last = pl.num_programs(1) - 1
    half = xlo_ref.shape[1]
    active = t < na_ref[0]

    @pl.when(active & (f == 0))
    def _():
        xlo_ref[...], xhi_ref[...] = _unpack_bf16_pairs(xs_ref[...])
        acc_ref[...] = jnp.zeros_like(acc_ref)

    @pl.when(active)
    def _():
        xlo, xhi = xlo_ref[...], xhi_ref[...]
        g = _dot(xlo, wg_ref[0, 0:half, :]) + _dot(xhi, wg_ref[0, half:, :])
        u = _dot(xlo, wu_ref[0, 0:half, :]) + _dot(xhi, wu_ref[0, half:, :])
        acc_ref[...] += _dot((_silu(g) * u).astype(BF16), wd_ref[0])

    @pl.when(active & (f == last))
    def _():
        y_ref[...] = acc_ref[...]

    @pl.when(jnp.logical_not(active) & (f == last))
    def _():
        y_ref[...] = jnp.zeros_like(y_ref)


def _experts(xs, tile_expert, tile_block, n_active, n_tiles, wg, wu, wd):
    rows, half = xs.shape
    _, d, d_ff = wg.shape
    tf = 512
    n_f = d_ff // tf
    spare = rows // EXPERT_TILE

    def live(t, na):
        return jnp.minimum(t, na[0] - 1)

    def up(t, f, te, tb, na):
        return (te[live(t, na)], 0, jnp.where(t < na[0], f, n_f - 1))

    def down(t, f, te, tb, na):
        return (te[live(t, na)], jnp.where(t < na[0], f, n_f - 1), 0)

    grid_spec = pltpu.PrefetchScalarGridSpec(
        num_scalar_prefetch=3,
        grid=(n_tiles, n_f),
        in_specs=[pl.BlockSpec((EXPERT_TILE, half), lambda t, f, te, tb, na: (tb[live(t, na)], 0)),
                  pl.BlockSpec((1, d, tf), up), pl.BlockSpec((1, d, tf), up), pl.BlockSpec((1, tf, d), down)],
        out_specs=pl.BlockSpec((EXPERT_TILE, d),
                               lambda t, f, te, tb, na: (jnp.where(t < na[0], tb[t], spare), 0)),
        scratch_shapes=[pltpu.VMEM((EXPERT_TILE, d), F32), pltpu.VMEM((EXPERT_TILE, half), BF16),
                        pltpu.VMEM((EXPERT_TILE, half), BF16)],
    )
    return pl.pallas_call(
        _experts_kernel,
        grid_spec=grid_spec,
        out_shape=jax.ShapeDtypeStruct((rows + EXPERT_TILE, d), F32),
        compiler_params=_params("arbitrary", "arbitrary"),
        name="experts",
    )(tile_expert, tile_block, n_active, xs, wg, wu, wd)


def _combine_kernel(*refs, in_tile, final):
    pos_ref, h_ref, route_ref, mod_ref = refs[:4]
    fnw_ref = refs[4] if final else None
    y_ref, out_ref, ybuf_ref, sem = refs[-4:]
    i = in_tile(pl.program_id(0), pl.program_id(1))
    rows = h_ref.shape[0]

    def issue(r, carry):
        for k in range(TOP_K):
            pltpu.make_async_copy(y_ref.at[pl.ds(pos_ref[i, TOP_K * r + k], 1), :],
                                  ybuf_ref.at[k, pl.ds(r, 1), :], sem).start(priority=k)
        return carry

    for r in range(rows):
        issue(r, 0)
    for k in range(TOP_K):
        pltpu.make_async_copy(y_ref.at[pl.ds(0, rows), :], ybuf_ref.at[k], sem).wait()
    y = route_ref[:, 4:5] * ybuf_ref[0] + route_ref[:, 5:6] * ybuf_ref[1]
    _residual_out(h_ref, mod_ref, y, fnw_ref, out_ref)


def _combine(y_sorted, pos, route, h, mods, final_norm_w, geo):
    n, d = h.shape
    final = final_norm_w is not None
    grid, in_tile, out_tile, out_rows = _tile_maps(geo, final)
    in_specs = [pl.BlockSpec((ROW_TILE, d), lambda b, j, pos: (in_tile(b, j), 0)),
                pl.BlockSpec((ROW_TILE, LANES), lambda b, j, pos: (in_tile(b, j), 0)),
                pl.BlockSpec((1, N_ADA, d), lambda b, j, pos: (b, 0, 0)) if final else
                pl.BlockSpec((1, N_ADA, d),
                             lambda b, j, pos: (jnp.where(j < geo["ctx_tiles"], geo["batch"], b), 0, 0))]
    args = [h, route, mods]
    if final:
        in_specs.append(pl.BlockSpec((1, d), lambda b, j, pos: (0, 0)))
        args.append(final_norm_w.reshape(1, d))
    in_specs.append(pl.BlockSpec(memory_space=pl.ANY))
    args.append(y_sorted)
    grid_spec = pltpu.PrefetchScalarGridSpec(
        num_scalar_prefetch=1,
        grid=grid,
        in_specs=in_specs,
        out_specs=pl.BlockSpec((ROW_TILE, d), lambda b, j, pos: (out_tile(b, j), 0)),
        scratch_shapes=[pltpu.VMEM((TOP_K, ROW_TILE, d), F32), pltpu.SemaphoreType.DMA(())],
    )
    return pl.pallas_call(
        functools.partial(_combine_kernel, in_tile=in_tile, final=final),
        grid_spec=grid_spec,
        out_shape=jax.ShapeDtypeStruct((out_rows, d), F32),
        compiler_params=_params("arbitrary", "arbitrary"),
        name="combine",
    )(pos, *args)


def _tile_schedule(cnt, n_tiles, cap):
    n_exp = cnt.shape[0]
    tiles = (cnt + EXPERT_TILE - 1) // EXPERT_TILE
    ends = jnp.cumsum(tiles)
    t = jnp.arange(n_tiles, dtype=I32)
    expert = jnp.minimum(jnp.sum(t[:, None] >= ends[None, :], axis=-1), n_exp - 1).astype(I32)
    first = jnp.sum(jnp.where(expert[:, None] == jnp.arange(n_exp, dtype=I32), ends - tiles, 0), axis=-1)
    block = expert * (cap // EXPERT_TILE) + (t - first)
    return expert, block.astype(I32), ends[-1:].astype(I32)


def _moe(xs, route, counts, h, mods, wg, wu, wd, first_expert, n_exp, final_norm_w, geo):
    n = h.shape[0]
    cap = _expert_capacity(n)
    n_tiles = -(-n * TOP_K // EXPERT_TILE) + n_exp
    cnt = counts[0, :n_exp].astype(I32)
    tile_expert, tile_block, n_active = _tile_schedule(cnt, n_tiles, cap)
    xs = _zero_padding(xs, cnt, cap)
    ys = _experts(xs, tile_expert + first_expert, tile_block, n_active, n_tiles, wg, wu, wd)
    pos = route[:, ROUTE_POS_LANE:ROUTE_POS_LANE + TOP_K].astype(I32).reshape(-1, TOP_K * ROW_TILE)
    return _combine(ys, pos, route, h, mods, final_norm_w, geo)


def kernel(x, c, ctx, c_ctx, w_ada, b_ada, norm_w, w_in, hgrn_lb_raw, hgrn_norm_w, w_pool, pool_scale, w_out,
           ffn_wg, ffn_wu, ffn_wd, router_w, moe_wg, moe_wu, moe_wd, final_norm_w):
    batch, seq_len, d = x.shape
    ctx_len = ctx.shape[1]
    depth = w_in.shape[0]
    seq = ctx_len + seq_len
    assert ctx_len % ROW_TILE == 0 and seq_len % ROW_TILE == 0 and seq_len % GRID_W == 0
    geo = dict(batch=batch, seq=seq, ctx_len=ctx_len, tpb=seq // ROW_TILE, ctx_tiles=ctx_len // ROW_TILE)

    lb_cum = jnp.cumsum(jax.nn.softmax(hgrn_lb_raw.astype(F32), axis=0), axis=0)
    lower_bounds = lb_cum - lb_cum[:1]

    cond_rows = -(-(batch + 1) // 8) * 8
    cond = jnp.concatenate([c, c_ctx[None, :], jnp.zeros((cond_rows - batch - 1, d), F32)], axis=0)
    mods_all = _modulation(cond, w_ada, b_ada).reshape(depth, cond_rows, N_ADA, d)

    n_exp = moe_wg.shape[1]
    moe_w = [w.astype(BF16).reshape((-1,) + w.shape[2:]) for w in (moe_wg, moe_wu, moe_wd)]

    h = (ctx.reshape(batch * ctx_len, d), x.reshape(batch * seq_len, d))
    for l in range(depth):
        mods = mods_all[l]
        final_w = final_norm_w if l == depth - 1 else None
        z = _mix_in(h, mods, norm_w[l, 0], w_in[l].astype(BF16), geo)
        o_fwd, o_bwd = _hgrn(z, lower_bounds[l], geo)
        pdiff = _pool(z, geo)
        dense = l % 2 == 0
        outs = _mix_out(o_fwd, o_bwd, z, pdiff, h, mods, hgrn_norm_w[l], w_pool[l].astype(BF16), pool_scale[l],
                        w_out[l].astype(BF16), norm_w[l, 1], None if dense else router_w[l // 2], geo)
        if dense:
            h1, v = outs
            h = _ffn(v, h1, mods, ffn_wg[l // 2].astype(BF16), ffn_wu[l // 2].astype(BF16),
                     ffn_wd[l // 2].astype(BF16), final_w, geo)
        else:
            h1, route, counts, xs = outs
            h = _moe(xs, route, counts, h1, mods, *moe_w, (l // 2) * n_exp, n_exp, final_w, geo)
    return h.reshape(batch, seq_len, d)
```

```python
import functools

import jax
import jax.numpy as jnp
from jax import lax
from jax.experimental import pallas as pl
from jax.experimental.pallas import tpu as pltpu

F32 = jnp.float32
BF16 = jnp.bfloat16
I32 = jnp.int32
U32 = jnp.uint32

HEAD_DIM = 128
GRID_W = 64
POOL_WINDOWS = (2, 4, 8, 16)
N_ADA = 6
TOP_K = 2
EPS = 1e-6

LANES = 128
ROW_TILE = 256
HGRN_CHUNK = 128
HGRN_MEMBERS = 2
EXPERT_TILE = 1024
HGRN_FAST_LIMIT = 160.0
VMEM_LIMIT_BYTES = 48 * 1024 * 1024

_NT = (((1,), (1,)), ((), ()))


def _params(*semantics):
    return pltpu.CompilerParams(dimension_semantics=semantics, vmem_limit_bytes=VMEM_LIMIT_BYTES)


def _dot(a, b):
    return jnp.dot(a, b, preferred_element_type=F32)


def _silu(x):
    return x * jax.nn.sigmoid(x)


def _rms(x, w):
    return x * lax.rsqrt(jnp.mean(x * x, axis=-1, keepdims=True) + EPS) * w


def _mod_row(tile, tiles_per_batch, ctx_tiles, batch):
    return jnp.where(tile % tiles_per_batch < ctx_tiles, batch, tile // tiles_per_batch)


def _mod_kernel(c_ref, w_ref, b_ref, o_ref):
    o_ref[0] = jnp.dot(_silu(c_ref[...]), w_ref[0], preferred_element_type=F32,
                       precision=lax.Precision.HIGHEST) + b_ref[0]


def _modulation(cond, w_ada, b_ada):
    depth, d, n_out = w_ada.shape
    rows = cond.shape[0]
    tn = 1024
    return pl.pallas_call(
        _mod_kernel,
        grid=(depth, n_out // tn),
        in_specs=[
            pl.BlockSpec((rows, d), lambda l, j: (0, 0)),
            pl.BlockSpec((1, d, tn), lambda l, j: (l, 0, j)),
            pl.BlockSpec((1, 1, tn), lambda l, j: (l, 0, j)),
        ],
        out_specs=pl.BlockSpec((1, rows, tn), lambda l, j: (l, 0, j)),
        out_shape=jax.ShapeDtypeStruct((depth, rows, n_out), F32),
        compiler_params=_params("arbitrary", "arbitrary"),
        name="modulation",
    )(cond, w_ada, b_ada.reshape(depth, 1, n_out))


def _stream_specs(h, geo):
    if not isinstance(h, tuple):
        return [pl.BlockSpec((ROW_TILE, h.shape[1]), lambda i: (i, 0))], [h]
    tpb, ct, batch = geo["tpb"], geo["ctx_tiles"], geo["batch"]
    lat = tpb - ct
    d = h[0].shape[1]

    def ctx_block(i):
        b, j = i // tpb, i % tpb
        return (jnp.where(j < ct, b * ct + j, jnp.minimum(b + 1, batch - 1) * ct), 0)

    def lat_block(i):
        b, j = i // tpb, i % tpb
        return (jnp.where(j < ct, b * lat, b * lat + j - ct), 0)

    return [pl.BlockSpec((ROW_TILE, d), ctx_block), pl.BlockSpec((ROW_TILE, d), lat_block)], list(h)


def _stream_tile(refs, geo):
    if len(refs) == 1:
        return refs[0][...]
    is_ctx = pl.program_id(0) % geo["tpb"] < geo["ctx_tiles"]
    return jnp.where(is_ctx, refs[0][...], refs[1][...])


def _mix_in_kernel(*refs, geo):
    mod_ref, nw_ref, w_ref, z_ref = refs[-4:]
    u = _rms(_stream_tile(refs[:-4], geo), nw_ref[...]) * (1 + mod_ref[0, 1:2, :]) + mod_ref[0, 0:1, :]
    z_ref[...] = _dot(u.astype(BF16), w_ref[...])


def _mix_in(h, mods, norm_w, w_in, geo):
    d, d_in = w_in.shape
    n = geo["batch"] * geo["seq"]
    row = functools.partial(_mod_row, tiles_per_batch=geo["tpb"], ctx_tiles=geo["ctx_tiles"], batch=geo["batch"])
    stream_specs, stream = _stream_specs(h, geo)
    return pl.pallas_call(
        functools.partial(_mix_in_kernel, geo=geo),
        grid=(n // ROW_TILE,),
        in_specs=stream_specs + [
            pl.BlockSpec((1, N_ADA, d), lambda i: (row(i), 0, 0)),
            pl.BlockSpec((1, d), lambda i: (0, 0)),
            pl.BlockSpec((d, d_in), lambda i: (0, 0)),
        ],
        out_specs=pl.BlockSpec((ROW_TILE, d_in), lambda i: (i, 0)),
        out_shape=jax.ShapeDtypeStruct((n, d_in), F32),
        compiler_params=_params("arbitrary"),
        name="mix_in",
    )(*stream, mods, norm_w.reshape(1, d), w_in)


def _split3(x):
    hi = x.astype(BF16)
    r1 = x - hi.astype(F32)
    mid = r1.astype(BF16)
    lo = (r1 - mid.astype(F32)).astype(BF16)
    return hi, mid, lo


def _chunk_cumsum(lg_ref, cum_ref, *, c, reverse):
    n = lg_ref.shape[0]
    t_idx = lax.broadcasted_iota(I32, (n, n), 0)
    s_idx = lax.broadcasted_iota(I32, (n, n), 1)
    incl = (s_idx >= t_idx) if reverse else (s_idx <= t_idx)
    same_chunk = (t_idx // c) == (s_idx // c)
    tri = jnp.where(jnp.logical_and(incl, same_chunk), 1.0, 0.0).astype(BF16)
    hi, mid, lo = _split3(lg_ref[...])
    cum_ref[...] = _dot(tri, hi) + _dot(tri, mid) + _dot(tri, lo)


def _hgrn_chunk(q_ref, v_ref, cum_ref, k_ref, o_ref, st_ref, row0, *, c, reverse, fast, heads):
    rows = pl.ds(row0, c)
    t_idx = lax.broadcasted_iota(I32, (c, c), 0)
    s_idx = lax.broadcasted_iota(I32, (c, c), 1)
    incl = (s_idx >= t_idx) if reverse else (s_idx <= t_idx)
    b = cum_ref[rows, :]
    b_end = b[0:1, :] if reverse else b[c - 1:c, :]
    q = q_ref[rows, :] * (HEAD_DIM ** -0.5)
    k = k_ref[rows, :]
    v = v_ref[rows, :]

    if fast:
        r = 0.5 * b_end
        er = jnp.exp(r)
        qt = (q * jnp.exp(b - r)).astype(BF16)
        kt = (k * jnp.exp(r - b)).astype(BF16)
        for h in range(heads):
            hs = slice(h * HEAD_DIM, (h + 1) * HEAD_DIM)
            scores = lax.dot_general(qt[:, hs], kt[:, hs], _NT, preferred_element_type=F32)
            p = jnp.where(incl, scores, 0.0).astype(BF16)
            st = st_ref[h] * er[:, hs]
            o = _dot(p, v[:, hs].astype(BF16)) + lax.dot_general(
                qt[:, hs], st.astype(BF16), _NT, preferred_element_type=F32)
            o_ref[rows, hs] = o
            st_ref[h] = (st + _dot(v[:, hs].T.astype(BF16), kt[:, hs])) * er[:, hs]
    else:
        qd = (q * jnp.exp(b)).astype(BF16)
        ke = (k * jnp.exp(b_end - b)).astype(BF16)
        decay = jnp.exp(b_end)
        for h in range(heads):
            hs = slice(h * HEAD_DIM, (h + 1) * HEAD_DIM)
            st = st_ref[h]
            o_ref[rows, hs] = lax.dot_general(qd[:, hs], st.astype(BF16), _NT, preferred_element_type=F32)
            st_ref[h] = st * decay[:, hs] + _dot(v[:, hs].T.astype(BF16), ke[:, hs])
        t_col = lax.broadcasted_iota(I32, (c, HEAD_DIM), 0)

        def key_step(s, carry):
            b_s = cum_ref[pl.ds(row0 + s, 1), :]
            k_s = k_ref[pl.ds(row0 + s, 1), :]
            v_s = v_ref[pl.ds(row0 + s, 1), :]
            prod = q * jnp.exp(jnp.minimum(b - b_s, 0.0)) * k_s
            visible = (t_col <= s) if reverse else (t_col >= s)
            for h in range(heads):
                hs = slice(h * HEAD_DIM, (h + 1) * HEAD_DIM)
                w = jnp.sum(prod[:, hs], axis=-1, keepdims=True)
                o_ref[rows, hs] += jnp.where(visible, w, 0.0) * v_s[:, hs]
            return carry

        lax.fori_loop(0, c, key_step, 0)


def _hgrn_kernel(qf_ref, vf_ref, ff_ref, qb_ref, vb_ref, fb_ref, lb_ref, of_ref, ob_ref,
                 stf_ref, stb_ref, lgf_ref, lgb_ref, kf_ref, kb_ref, cumf_ref, cumb_ref, *, heads):
    members, block = qf_ref.shape[0], qf_ref.shape[1]
    half = HGRN_CHUNK // 2
    n_half = block // half

    @pl.when(pl.program_id(1) == 0)
    def _():
        stf_ref[...] = jnp.zeros_like(stf_ref)
        stb_ref[...] = jnp.zeros_like(stb_ref)

    worst_full = jnp.float32(0.0)
    worst_half = jnp.float32(0.0)
    for m in range(members):
        for d, (f_ref, lg_ref, k_ref) in enumerate(((ff_ref, lgf_ref, kf_ref), (fb_ref, lgb_ref, kb_ref))):
            lb = lb_ref[d:d + 1, :]
            f = lb + (1 - lb) * jax.nn.sigmoid(f_ref[m])
            lg = jnp.log(f)
            lg_ref[m] = lg
            k_ref[m] = 1 - f
            half_decay = -jnp.sum(lg.reshape(n_half, half, lg.shape[-1]), axis=1)
            worst_half = jnp.maximum(worst_half, jnp.max(half_decay))
            for i in range(0, n_half, 2):
                worst_full = jnp.maximum(worst_full, jnp.max(half_decay[i:i + 1] + half_decay[i + 1:i + 2]))
    full_ok = worst_full < HGRN_FAST_LIMIT
    half_ok = jnp.logical_and(jnp.logical_not(full_ok), worst_half < HGRN_FAST_LIMIT)
    neither = jnp.logical_and(jnp.logical_not(full_ok), jnp.logical_not(worst_half < HGRN_FAST_LIMIT))

    def run(c, fast):
        n_chunks = block // c
        for m in range(members):
            _chunk_cumsum(lgf_ref.at[m], cumf_ref.at[m], c=c, reverse=False)
            _chunk_cumsum(lgb_ref.at[m], cumb_ref.at[m], c=c, reverse=True)

        def body(ci, carry):
            fwd_row = ci * c
            bwd_row = (n_chunks - 1 - ci) * c
            if not isinstance(ci, int):
                fwd_row, bwd_row = pl.multiple_of(fwd_row, c), pl.multiple_of(bwd_row, c)
            for m in range(members):
                _hgrn_chunk(qf_ref.at[m], vf_ref.at[m], cumf_ref.at[m], kf_ref.at[m], of_ref.at[m],
                            stf_ref.at[m], fwd_row, c=c, reverse=False, fast=fast, heads=heads)
                _hgrn_chunk(qb_ref.at[m], vb_ref.at[m], cumb_ref.at[m], kb_ref.at[m], ob_ref.at[m],
                            stb_ref.at[m], bwd_row, c=c, reverse=True, fast=fast, heads=heads)
            return carry

        if fast:
            for ci in range(n_chunks):
                body(ci, 0)
        else:
            lax.fori_loop(0, n_chunks, body, 0)

    @pl.when(full_ok)
    def _():
        run(HGRN_CHUNK, True)

    @pl.when(half_ok)
    def _():
        run(half, True)

    @pl.when(neither)
    def _():
        run(HGRN_CHUNK, False)


def _hgrn(z, lower_bounds, geo):
    n = z.shape[0]
    d_h = lower_bounds.shape[-1]
    heads = d_h // HEAD_DIM
    nblk, ctx_blocks, batch, seq = geo["tpb"], geo["ctx_tiles"], geo["batch"], geo["seq"]
    members = HGRN_MEMBERS if batch % HGRN_MEMBERS == 0 else 1
    z3 = z.reshape(batch, seq, z.shape[1])

    def fwd_block(j):
        return j

    def bwd_block(j):
        return jnp.where(j < ctx_blocks, ctx_blocks - 1 - j, nblk - 1 - (j - ctx_blocks))

    def col(block_fn, c):
        return pl.BlockSpec((members, ROW_TILE, d_h), lambda b, j: (b, block_fn(j), c))

    state = pltpu.VMEM((members, heads, HEAD_DIM, HEAD_DIM), F32)
    rows = pltpu.VMEM((members, ROW_TILE, d_h), F32)
    o_fwd, o_bwd = pl.pallas_call(
        functools.partial(_hgrn_kernel, heads=heads),
        grid=(batch // members, nblk),
        in_specs=[col(fwd_block, 0), col(fwd_block, 1), col(fwd_block, 2),
                  col(bwd_block, 0), col(bwd_block, 1), col(bwd_block, 3),
                  pl.BlockSpec((2, d_h), lambda b, j: (0, 0))],
        out_specs=[col(fwd_block, 0), col(bwd_block, 0)],
        out_shape=[jax.ShapeDtypeStruct((batch, seq, d_h), F32)] * 2,
        scratch_shapes=[state, state, rows, rows, rows, rows, rows, rows],
        compiler_params=_params("arbitrary", "arbitrary"),
        name="hgrn",
    )(z3, z3, z3, z3, z3, z3, lower_bounds)
    return o_fwd.reshape(n, d_h), o_bwd.reshape(n, d_h)


def _window_sum(x, k):
    n = x.shape[0]
    t = lax.broadcasted_iota(I32, x.shape, 0)

    def ahead(a, d):
        return jnp.where(t + d < n, pltpu.roll(a, n - d, axis=0), 0.0)

    def behind(a, d):
        return jnp.where(t >= d, pltpu.roll(a, d, axis=0), 0.0)

    fwd = x
    bwd = behind(x, 1)
    w = 1
    while 2 * w <= k // 2:
        fwd = fwd + ahead(fwd, w)
        bwd = bwd + behind(bwd, w)
        w *= 2
    return fwd + bwd


def _window_count(shape, n, k, offset=0):
    t = lax.broadcasted_iota(I32, shape, 0) + offset
    lo, hi = k // 2, k - 1 - k // 2
    return (jnp.minimum(t + hi + 1, n) - jnp.maximum(t - lo, 0)).astype(F32)


def _pool_group(x_ref, o_ref, cs_ref, k, ctx_len, grid_rows):
    lo, hi = k // 2, k - 1 - k // 2
    pad = POOL_WINDOWS[-1] // 2

    x = x_ref[0:ctx_len, :]
    mean = _window_sum(x, k) / _window_count(x.shape, ctx_len, k)
    o_ref[0:ctx_len, :] = (mean - x).astype(o_ref.dtype)

    zeros = jnp.zeros((pad * GRID_W, LANES), F32)
    cs_ref[0:pad * GRID_W, :] = zeros
    cs_ref[(pad + grid_rows) * GRID_W:(2 * pad + grid_rows) * GRID_W, :] = zeros

    def col_pass(r, carry):
        src = pl.multiple_of(ctx_len + r * GRID_W, GRID_W)
        dst = pl.multiple_of((pad + r) * GRID_W, GRID_W)
        cs_ref[pl.ds(dst, GRID_W), :] = _window_sum(x_ref[pl.ds(src, GRID_W), :], k)
        return carry

    lax.fori_loop(0, grid_rows, col_pass, 0)
    n_col = _window_count((GRID_W, LANES), GRID_W, k)

    def row_pass(r, carry):
        acc = jnp.zeros((GRID_W, LANES), F32)
        for dr in range(-lo, hi + 1):
            acc = acc + cs_ref[pl.ds(pl.multiple_of((pad + r + dr) * GRID_W, GRID_W), GRID_W), :]
        n_row = (jnp.minimum(r + hi + 1, grid_rows) - jnp.maximum(r - lo, 0)).astype(F32)
        src = pl.multiple_of(ctx_len + r * GRID_W, GRID_W)
        o_ref[pl.ds(src, GRID_W), :] = (acc / (n_row * n_col) - x_ref[pl.ds(src, GRID_W), :]).astype(o_ref.dtype)
        return carry

    lax.fori_loop(0, grid_rows, row_pass, 0)


def _pool_kernel(x_ref, o_ref, cs_ref, *, ctx_len, grid_rows):
    group = pl.program_id(1)
    for gi, k in enumerate(POOL_WINDOWS):
        @pl.when(group == gi)
        def _(k=k):
            _pool_group(x_ref, o_ref, cs_ref, k, ctx_len, grid_rows)


def _pool(z, geo):
    n, d_in = z.shape
    groups = len(POOL_WINDOWS)
    first = d_in // LANES - groups
    seq, ctx_len = geo["seq"], geo["ctx_len"]
    grid_rows = (seq - ctx_len) // GRID_W
    pad = POOL_WINDOWS[-1] // 2
    return pl.pallas_call(
        functools.partial(_pool_kernel, ctx_len=ctx_len, grid_rows=grid_rows),
        grid=(geo["batch"], groups),
        in_specs=[pl.BlockSpec((seq, LANES), lambda b, g: (b, first + g))],
        out_specs=pl.BlockSpec((seq, LANES), lambda b, g: (b, g)),
        out_shape=jax.ShapeDtypeStruct((n, groups * LANES), BF16),
        scratch_shapes=[pltpu.VMEM(((grid_rows + 2 * pad) * GRID_W, LANES), F32)],
        compiler_params=_params("arbitrary", "arbitrary"),
        name="pool",
    )(z)


def _pack_bf16_pairs(x):
    w = x.shape[1] // 2
    lo = lax.bitcast_convert_type(x[:, :w].astype(BF16).astype(F32), U32)
    hi = lax.bitcast_convert_type(x[:, w:].astype(BF16).astype(F32), U32)
    return (lo >> 16) | (hi & jnp.uint32(0xFFFF0000))


def _unpack_bf16_pairs(w):
    lo = lax.bitcast_convert_type(w << 16, F32).astype(BF16)
    hi = lax.bitcast_convert_type(w & jnp.uint32(0xFFFF0000), F32).astype(BF16)
    return lo, hi


ROUTE_POS_LANE = 6


def _route(v, rw_ref, carry_ref, n_exp, cap):
    rows = v.shape[0]
    v_hi = v.astype(BF16)
    v_lo = (v - v_hi.astype(F32)).astype(BF16)
    hi_prod = _dot(v_hi, rw_ref[...])
    logits = hi_prod[:, :LANES] + hi_prod[:, LANES:] + _dot(v_lo, rw_ref[:, :LANES])
    lane = lax.broadcasted_iota(I32, (rows, LANES), 1).astype(F32)
    neg = jnp.float32(-jnp.inf)
    logits = jnp.where(lane < n_exp, logits, neg)
    m1 = jnp.max(logits, axis=-1, keepdims=True)
    i1 = jnp.min(jnp.where(logits == m1, lane, float(LANES)), axis=-1, keepdims=True)
    rest = jnp.where(lane == i1, neg, logits)
    m2 = jnp.max(rest, axis=-1, keepdims=True)
    i2 = jnp.min(jnp.where(rest == m2, lane, float(LANES)), axis=-1, keepdims=True)
    e = jnp.exp(m2 - m1)
    w1 = 1.0 / (1.0 + e)
    w2 = e / (1.0 + e)
    chosen = jnp.where((lane == i1) | (lane == i2), 1.0, 0.0)
    t_idx = lax.broadcasted_iota(I32, (rows, rows), 0)
    s_idx = lax.broadcasted_iota(I32, (rows, rows), 1)
    before = jnp.where(s_idx < t_idx, 1.0, 0.0).astype(BF16)
    slots = _dot(before, chosen.astype(BF16)) + carry_ref[...]
    r1 = jnp.sum(jnp.where(lane == i1, slots, 0.0), axis=-1, keepdims=True)
    r2 = jnp.sum(jnp.where(lane == i2, slots, 0.0), axis=-1, keepdims=True)
    carry_ref[...] += jnp.sum(chosen, axis=0, keepdims=True)
    out = jnp.zeros((rows, LANES), F32)
    for idx, val in enumerate((i1, i2, r1, r2, w1, w2, i1 * cap + r1, i2 * cap + r2)):
        out = jnp.where(lane == idx, val, out)
    return out


def _mixer_tail(of_ref, ob_ref, g_ref, pd_ref, h, mod_ref, hnw_ref, wp_ref, ps_ref, wo_ref, nw_ref, *,
                heads, groups):
    d_h = heads * HEAD_DIM
    o = of_ref[...] + ob_ref[...]
    normed = []
    for hd in range(heads):
        oh = o[:, hd * HEAD_DIM:(hd + 1) * HEAD_DIM]
        normed.append(oh * lax.rsqrt(jnp.mean(oh * oh, axis=-1, keepdims=True) + EPS))
    a = jnp.concatenate(normed, axis=-1) * hnw_ref[...] * _silu(g_ref[...])
    y = _dot(a.astype(BF16), wo_ref[0:d_h, :])
    for gi in range(groups):
        gs = slice(gi * LANES, (gi + 1) * LANES)
        p = _dot(pd_ref[:, gs], wp_ref[gi]) * ps_ref[:, gs]
        y = y + _dot(p.astype(BF16), wo_ref[d_h + gi * LANES:d_h + (gi + 1) * LANES, :])
    h1 = h + mod_ref[0, 2:3, :] * y
    v = _rms(h1, nw_ref[...]) * (1 + mod_ref[0, 4:5, :]) + mod_ref[0, 3:4, :]
    return h1, v


def _mix_out_kernel(*refs, heads, groups, geo):
    h1_ref, v_ref = refs[-2:]
    stream_refs = refs[4:-8]
    h1, v = _mixer_tail(*refs[:4], _stream_tile(stream_refs, geo), *refs[-8:-2], heads=heads, groups=groups)
    h1_ref[...] = h1
    v_ref[...] = v.astype(BF16)


def _mix_out_route_kernel(*refs, heads, groups, n_exp, cap):
    rw_ref, h1_ref, route_ref, cnt_ref, xs_ref = refs[11:16]
    carry_ref, vp0, vp1, pv0, pv1, ps0, ps1, marker_ref, pos_sem, row_sem, marker_sem = refs[16:]
    vp, pos_vmem, pos_smem = (vp0, vp1), (pv0, pv1), (ps0, ps1)
    rows = vp0.shape[0]
    i = pl.program_id(0)
    n_tiles = pl.num_programs(0) - 1

    def pos_copy(slot):
        return pltpu.make_async_copy(pos_vmem[slot], pos_smem[slot], pos_sem)

    def scatter(slot):
        pos_copy(slot).wait()
        marker = pltpu.make_async_copy(marker_ref.at[0], marker_ref.at[1], marker_sem)
        marker.start()
        for r in range(rows):
            for k in range(TOP_K):
                pltpu.make_async_copy(vp[slot].at[pl.ds(r, 1), :],
                                      xs_ref.at[pl.ds(pos_smem[slot][ROUTE_POS_LANE + k, r], 1), :],
                                      row_sem).start(priority=k)
        marker.wait()

    def scatter_wait(slot):
        for _ in range(TOP_K):
            pltpu.make_async_copy(vp[slot], xs_ref.at[pl.ds(0, rows), :], row_sem).wait()

    def compute(slot):
        h1, v = _mixer_tail(*refs[:4], refs[4][...], *refs[5:11], heads=heads, groups=groups)
        h1_ref[...] = h1
        route =_route(v, rw_ref, carry_ref, n_exp, cap)
        route_ref[...] = route
        cnt_ref[...] = jnp.broadcast_to(carry_ref[...], cnt_ref.shape)
        vp[slot][...] = _pack_bf16_pairs(v)
        pos_vmem[slot][...] = route.T[0:pos_vmem[slot].shape[0], :].astype(I32)
        pos_copy(slot).start()

    @pl.when(i == 0)
    def _():
        carry_ref[...] = jnp.zeros_like(carry_ref)
        marker_ref[...] = jnp.zeros_like(marker_ref)
        compute(0)

    for slot in (0, 1):
        @pl.when((i > 0) & (i < n_tiles) & (i % 2 == slot))
        def _(slot=slot):
            scatter(1 - slot)
            compute(slot)
            scatter_wait(1 - slot)

        @pl.when((i == n_tiles) & (i % 2 == slot))
        def _(slot=slot):
            scatter(1 - slot)
            scatter_wait(1 - slot)


def _mix_out(o_fwd, o_bwd, z, pdiff, h, mods, hgrn_norm_w, w_pool, pool_scale, w_out, norm_w, router_w, geo):
    n, d = o_fwd.shape[0], w_out.shape[1]
    d_h = o_fwd.shape[1]
    groups = w_pool.shape[0]
    d_p = groups * LANES
    n_tiles = n // ROW_TILE
    routed = router_w is not None
    clamp = (lambda i: jnp.minimum(i, n_tiles - 1)) if routed else (lambda i: i)
    row = functools.partial(_mod_row, tiles_per_batch=geo["tpb"], ctx_tiles=geo["ctx_tiles"], batch=geo["batch"])
    tile = lambda width, c=0: pl.BlockSpec((ROW_TILE, width), lambda i: (clamp(i), c))
    whole = lambda shape: pl.BlockSpec(shape, lambda i: (0,) * len(shape))
    stream_specs, stream = ([tile(d)], [h]) if routed else _stream_specs(h, geo)
    in_specs = [tile(d_h), tile(d_h), tile(d_h, 4), tile(d_p), *stream_specs,
                pl.BlockSpec((1, N_ADA, d), lambda i: (row(clamp(i)), 0, 0)),
                whole((1, d_h)), whole((groups, LANES, LANES)), whole((1, d_p)), whole((d_h + d_p, d)),
                whole((1, d))]
    args = [o_fwd, o_bwd, z, pdiff, *stream, mods, hgrn_norm_w.reshape(1, d_h), w_pool,
            pool_scale.reshape(1, d_p), w_out, norm_w.reshape(1, d)]
    heads = d_h // HEAD_DIM
    if not routed:
        return pl.pallas_call(
            functools.partial(_mix_out_kernel, heads=heads, groups=groups, geo=geo),
            grid=(n_tiles,),
            in_specs=in_specs,
            out_specs=[tile(d), tile(d)],
            out_shape=[jax.ShapeDtypeStruct((n, d), F32), jax.ShapeDtypeStruct((n, d), BF16)],
            compiler_params=_params("arbitrary"),
            name="mix_out",
        )(*args)

    n_exp = router_w.shape[1]
    cap = _expert_capacity(n)
    rw = jnp.pad(router_w.astype(F32), ((0, 0), (0, LANES - n_exp)))
    rw_hi = rw.astype(BF16)
    args.append(jnp.concatenate([rw_hi, (rw - rw_hi.astype(F32)).astype(BF16)], axis=1))
    in_specs.append(whole((d, 2 * LANES)))
    packed = pltpu.VMEM((ROW_TILE, d // 2), U32)
    return pl.pallas_call(
        functools.partial(_mix_out_route_kernel, heads=heads, groups=groups, n_exp=n_exp, cap=cap),
        grid=(n_tiles + 1,),
        in_specs=in_specs,
        out_specs=[tile(d), tile(LANES), whole((8, LANES)), pl.BlockSpec(memory_space=pl.ANY)],
        out_shape=[jax.ShapeDtypeStruct((n, d), F32), jax.ShapeDtypeStruct((n, LANES), F32),
                   jax.ShapeDtypeStruct((8, LANES), F32), jax.ShapeDtypeStruct((n_exp * cap, d // 2), U32)],
        scratch_shapes=[pltpu.VMEM((1, LANES), F32), packed, packed,
                        pltpu.VMEM((8, ROW_TILE), I32), pltpu.VMEM((8, ROW_TILE), I32),
                        pltpu.SMEM((8, ROW_TILE), I32), pltpu.SMEM((8, ROW_TILE), I32),
                        pltpu.VMEM((2, 8, LANES), I32),
                        pltpu.SemaphoreType.DMA(()), pltpu.SemaphoreType.DMA(()), pltpu.SemaphoreType.DMA(())],
        compiler_params=_params("arbitrary"),
        name="mix_out_route",
    )(*args)


def _expert_capacity(n_tokens):
    return -(-n_tokens // EXPERT_TILE) * EXPERT_TILE


def _residual_out(h_ref, mod_ref, y, fnw_ref, out_ref):
    h2 = h_ref[...] + mod_ref[0, 5:6, :] * y
    out_ref[...] = h2 if fnw_ref is None else _rms(h2, fnw_ref[...])


def _ffn_kernel(*refs, final):
    x_ref, h_ref, mod_ref, wg_ref, wu_ref, wd_ref = refs[:6]
    fnw_ref = refs[6] if final else None
    out_ref = refs[-1]
    x = x_ref[...]
    a = (_silu(_dot(x, wg_ref[...])) * _dot(x, wu_ref[...])).astype(BF16)
    _residual_out(h_ref, mod_ref, _dot(a, wd_ref[...]), fnw_ref, out_ref)


def _tile_maps(geo, final):
    tpb, ctx_tiles, batch = geo["tpb"], geo["ctx_tiles"], geo["batch"]
    if final:
        lat = tpb - ctx_tiles
        return (batch, lat), (lambda b, j: b * tpb + ctx_tiles + j), (lambda b, j: b * lat + j), batch * lat * ROW_TILE
    return (batch, tpb), (lambda b, j: b * tpb + j), (lambda b, j: b * tpb + j), batch * tpb * ROW_TILE


def _ffn(v, h, mods, wg, wu, wd, final_norm_w, geo):
    n, d = h.shape
    d_ff = wg.shape[1]
    final = final_norm_w is not None
    grid, in_tile, out_tile, out_rows = _tile_maps(geo, final)
    whole = lambda shape: pl.BlockSpec(shape, lambda b, j: (0,) * len(shape))
    in_specs = [pl.BlockSpec((ROW_TILE, d), lambda b, j: (in_tile(b, j), 0)),
                pl.BlockSpec((ROW_TILE, d), lambda b, j: (in_tile(b, j), 0)),
                pl.BlockSpec((1, N_ADA, d), lambda b, j: (b, 0, 0)) if final else
                pl.BlockSpec((1, N_ADA, d), lambda b, j: (jnp.where(j < geo["ctx_tiles"], geo["batch"], b), 0, 0)),
                whole((d, d_ff)), whole((d, d_ff)), whole((d_ff, d))]
    args = [v, h, mods, wg, wu, wd]
    if final:
        in_specs.append(whole((1, d)))
        args.append(final_norm_w.reshape(1, d))
    return pl.pallas_call(
        functools.partial(_ffn_kernel, final=final),
        grid=grid,
        in_specs=in_specs,
        out_specs=pl.BlockSpec((ROW_TILE, d), lambda b, j: (out_tile(b, j), 0)),
        out_shape=jax.ShapeDtypeStruct((out_rows, d), F32),
        compiler_params=_params("arbitrary", "arbitrary"),
        name="ffn",
    )(*args)


def _zero_padding_kernel(cnt_ref, xs_in_ref, xs_ref, zero_ref, sem, *, cap):
    del xs_in_ref
    zero_ref[...] = jnp.zeros_like(zero_ref)

    def row_copy(e, r):
        return pltpu.make_async_copy(zero_ref.at[pl.ds(0, 1), :], xs_ref.at[pl.ds(e * cap + r, 1), :], sem)

    for e in range(cnt_ref.shape[0]):
        used = cnt_ref[e]
        tile_end = (used + EXPERT_TILE - 1) // EXPERT_TILE * EXPERT_TILE
        lax.fori_loop(used, tile_end, lambda r, c, e=e: (row_copy(e, r).start(), c)[1], 0)
        lax.fori_loop(used, tile_end, lambda r, c, e=e: (row_copy(e, r).wait(), c)[1], 0)


def _zero_padding(xs, cnt, cap):
    grid_spec = pltpu.PrefetchScalarGridSpec(
        num_scalar_prefetch=1,
        grid=(1,),
        in_specs=[pl.BlockSpec(memory_space=pl.ANY)],
        out_specs=pl.BlockSpec(memory_space=pl.ANY),
        scratch_shapes=[pltpu.VMEM((8, xs.shape[1]), U32), pltpu.SemaphoreType.DMA(())],
    )
    return pl.pallas_call(
        functools.partial(_zero_padding_kernel, cap=cap),
        grid_spec=grid_spec,
        out_shape=jax.ShapeDtypeStruct(xs.shape, xs.dtype),
        input_output_aliases={1: 0},
        compiler_params=_params("arbitrary"),
        name="zero_padding",
    )(cnt, xs)


def _experts_kernel(te_ref, tb_ref, na_ref, xs_ref, wg_ref, wu_ref, wd_ref, y_ref, acc_ref, xlo_ref, xhi_ref):
    t, f = pl.program_id(0), pl.program_id(1)
    last = pl.num_programs(1) - 1
    half = xlo_ref.shape[1]
    active = t < na_ref[0]

    @pl.when(active & (f == 0))
    def _():
        xlo_ref[...], xhi_ref[...] = _unpack_bf16_pairs(xs_ref[...])
        acc_ref[...] = jnp.zeros_like(acc_ref)

    @pl.when(active)
    def _():
        xlo, xhi = xlo_ref[...], xhi_ref[...]
        g = _dot(xlo, wg_ref[0, 0:half, :]) + _dot(xhi, wg_ref[0, half:, :])
        u = _dot(xlo, wu_ref[0, 0:half, :]) + _dot(xhi, wu_ref[0, half:, :])
        acc_ref[...] += _dot((_silu(g) * u).astype(BF16), wd_ref[0])

    @pl.when(active & (f == last))
    def _():
        y_ref[...] = acc_ref[...]

    @pl.when(jnp.logical_not(active) & (f == last))
    def _():
        y_ref[...] = jnp.zeros_like(y_ref)


def _experts(xs, tile_expert, tile_block, n_active, n_tiles, wg, wu, wd):
    rows, half = xs.shape
    _, d, d_ff = wg.shape
    tf = 512
    n_f = d_ff // tf
    spare = rows // EXPERT_TILE

    def live(t, na):
        return jnp.minimum(t, na[0] - 1)

    def up(t, f, te, tb, na):
        return (te[live(t, na)], 0, jnp.where(t < na[0], f, n_f - 1))

    def down(t, f, te, tb, na):
        return (te[live(t, na)], jnp.where(t < na[0], f, n_f - 1), 0)

    grid_spec = pltpu.PrefetchScalarGridSpec(
        num_scalar_prefetch=3,
        grid=(n_tiles, n_f),
        in_specs=[pl.BlockSpec((EXPERT_TILE, half), lambda t, f, te, tb, na: (tb[live(t, na)], 0)),
                  pl.BlockSpec((1, d, tf), up), pl.BlockSpec((1, d, tf), up), pl.BlockSpec((1, tf, d), down)],
        out_specs=pl.BlockSpec((EXPERT_TILE, d),
                               lambda t, f, te, tb, na: (jnp.where(t < na[0], tb[t], spare), 0)),
        scratch_shapes=[pltpu.VMEM((EXPERT_TILE, d), F32), pltpu.VMEM((EXPERT_TILE, half), BF16),
                        pltpu.VMEM((EXPERT_TILE, half), BF16)],
    )
    return pl.pallas_call(
        _experts_kernel,
        grid_spec=grid_spec,
        out_shape=jax.ShapeDtypeStruct((rows + EXPERT_TILE, d), F32),
        compiler_params=_params("arbitrary", "arbitrary"),
        name="experts",
    )(tile_expert, tile_block, n_active, xs, wg, wu, wd)


def _combine_kernel(*refs, in_tile, final):
    pos_ref, h_ref, route_ref, mod_ref = refs[:4]
    fnw_ref = refs[4] if final else None
    y_ref, out_ref, ybuf0, ybuf1, marker_ref, sem0, sem1, marker_sem = refs[-8:]
    ybuf, sems = (ybuf0, ybuf1), (sem0, sem1)
    inner = pl.num_programs(1)
    step = pl.program_id(0) * inner + pl.program_id(1)
    n_steps = pl.num_programs(0) * inner
    rows = h_ref.shape[0]

    def gather(s, slot):
        tile = in_tile(s // inner, s % inner)
        for r in range(rows):
            for k in range(TOP_K):
                pltpu.make_async_copy(y_ref.at[pl.ds(pos_ref[tile, TOP_K * r + k], 1), :],
                                      ybuf[slot].at[k, pl.ds(r, 1), :], sems[slot]).start(priority=k)

    def gather_wait(slot):
        for k in range(TOP_K):
            pltpu.make_async_copy(y_ref.at[pl.ds(0, rows), :], ybuf[slot].at[k], sems[slot]).wait()

    @pl.when(step == 0)
    def _():
        marker_ref[...] = jnp.zeros_like(marker_ref)
        gather(step, 0)

    for slot in (0, 1):
        @pl.when(step % 2 == slot)
        def _(slot=slot):
            gather_wait(slot)

            @pl.when(step + 1 < n_steps)
            def _():
                marker = pltpu.make_async_copy(marker_ref.at[0], marker_ref.at[1], marker_sem)
                marker.start()
                gather(step + 1, 1 - slot)
                marker.wait()

            y = route_ref[:, 4:5] * ybuf[slot][0] + route_ref[:, 5:6] * ybuf[slot][1]
            _residual_out(h_ref, mod_ref, y, fnw_ref, out_ref)


def _combine(y_sorted, pos, route, h, mods, final_norm_w, geo):
    n, d = h.shape
    final = final_norm_w is not None
    grid, in_tile, out_tile, out_rows = _tile_maps(geo, final)
    in_specs = [pl.BlockSpec((ROW_TILE, d), lambda b, j, pos: (in_tile(b, j), 0)),
                pl.BlockSpec((ROW_TILE, LANES), lambda b, j, pos: (in_tile(b, j), 0)),
                pl.BlockSpec((1, N_ADA, d), lambda b, j, pos: (b, 0, 0)) if final else
                pl.BlockSpec((1, N_ADA, d),
                             lambda b, j, pos: (jnp.where(j < geo["ctx_tiles"], geo["batch"], b), 0, 0))]
    args = [h, route, mods]
    if final:
        in_specs.append(pl.BlockSpec((1, d), lambda b, j, pos: (0, 0)))
        args.append(final_norm_w.reshape(1, d))
    in_specs.append(pl.BlockSpec(memory_space=pl.ANY))
    args.append(y_sorted)
    grid_spec = pltpu.PrefetchScalarGridSpec(
        num_scalar_prefetch=1,
        grid=grid,
        in_specs=in_specs,
        out_specs=pl.BlockSpec((ROW_TILE, d), lambda b, j, pos: (out_tile(b, j), 0)),
        scratch_shapes=[pltpu.VMEM((TOP_K, ROW_TILE, d), F32), pltpu.VMEM((TOP_K, ROW_TILE, d), F32),
                        pltpu.VMEM((2, 8, LANES), I32),
                        pltpu.SemaphoreType.DMA(()), pltpu.SemaphoreType.DMA(()), pltpu.SemaphoreType.DMA(())],
    )
    return pl.pallas_call(
        functools.partial(_combine_kernel, in_tile=in_tile, final=final),
        grid_spec=grid_spec,
        out_shape=jax.ShapeDtypeStruct((out_rows, d), F32),
        compiler_params=_params("arbitrary", "arbitrary"),
        name="combine",
    )(pos, *args)


def _tile_schedule(cnt, n_tiles, cap):
    n_exp = cnt.shape[0]
    tiles = (cnt + EXPERT_TILE - 1) // EXPERT_TILE
    ends = jnp.cumsum(tiles)
    t = jnp.arange(n_tiles, dtype=I32)
    expert = jnp.minimum(jnp.sum(t[:, None] >= ends[None, :], axis=-1), n_exp - 1).astype(I32)
    first = jnp.sum(jnp.where(expert[:, None] == jnp.arange(n_exp, dtype=I32), ends - tiles, 0), axis=-1)
    block = expert * (cap // EXPERT_TILE) + (t - first)
    return expert, block.astype(I32), ends[-1:].astype(I32)


def _moe(xs, route, counts, h, mods, wg, wu, wd, first_expert, n_exp, final_norm_w, geo):
    n = h.shape[0]
    cap = _expert_capacity(n)
    n_tiles = -(-n * TOP_K // EXPERT_TILE) + n_exp
    cnt = counts[0, :n_exp].astype(I32)
    tile_expert, tile_block, n_active = _tile_schedule(cnt, n_tiles, cap)
    xs = _zero_padding(xs, cnt, cap)
    ys = _experts(xs, tile_expert + first_expert, tile_block, n_active, n_tiles, wg, wu, wd)
    pos = route[:, ROUTE_POS_LANE:ROUTE_POS_LANE + TOP_K].astype(I32).reshape(-1, TOP_K * ROW_TILE)
    return _combine(ys, pos, route, h, mods, final_norm_w, geo)


def kernel(x, c, ctx, c_ctx, w_ada, b_ada, norm_w, w_in, hgrn_lb_raw, hgrn_norm_w, w_pool, pool_scale, w_out,
           ffn_wg, ffn_wu, ffn_wd, router_w, moe_wg, moe_wu, moe_wd, final_norm_w):
    batch, seq_len, d = x.shape
    ctx_len = ctx.shape[1]
    depth = w_in.shape[0]
    seq = ctx_len + seq_len
    assert ctx_len % ROW_TILE == 0 and seq_len % ROW_TILE == 0 and seq_len % GRID_W == 0
    geo = dict(batch=batch, seq=seq, ctx_len=ctx_len, tpb=seq // ROW_TILE, ctx_tiles=ctx_len // ROW_TILE)

    lb_cum = jnp.cumsum(jax.nn.softmax(hgrn_lb_raw.astype(F32), axis=0), axis=0)
    lower_bounds = lb_cum - lb_cum[:1]

    cond_rows = -(-(batch + 1) // 8) * 8
    cond = jnp.concatenate([c, c_ctx[None, :], jnp.zeros((cond_rows - batch - 1, d), F32)], axis=0)
    mods_all = _modulation(cond, w_ada, b_ada).reshape(depth, cond_rows, N_ADA, d)

    n_exp = moe_wg.shape[1]
    moe_w = [w.astype(BF16).reshape((-1,) + w.shape[2:]) for w in (moe_wg, moe_wu, moe_wd)]

    h = (ctx.reshape(batch * ctx_len, d), x.reshape(batch * seq_len, d))
    for l in range(depth):
        mods = mods_all[l]
        final_w = final_norm_w if l == depth - 1 else None
        z = _mix_in(h, mods, norm_w[l, 0], w_in[l].astype(BF16), geo)
        o_fwd, o_bwd = _hgrn(z, lower_bounds[l], geo)
        pdiff = _pool(z, geo)
        dense = l % 2 == 0
        outs = _mix_out(o_fwd, o_bwd, z, pdiff, h, mods, hgrn_norm_w[l], w_pool[l].astype(BF16), pool_scale[l],
                        w_out[l].astype(BF16), norm_w[l, 1], None if dense else router_w[l // 2], geo)
        if dense:
            h1, v = outs
            h = _ffn(v, h1, mods, ffn_wg[l // 2].astype(BF16), ffn_wu[l // 2].astype(BF16),
                     ffn_wd[l // 2].astype(BF16), final_w, geo)
        else:
            h1, route, counts, xs = outs
            h = _moe(xs, route, counts, h1, mods, *moe_w, (l // 2) * n_exp, n_exp, final_w, geo)
    return h.reshape(batch, seq_len, d)
```

```python
import functools

import jax
import jax.numpy as jnp
from jax import lax
from jax.experimental import pallas as pl
from jax.experimental.pallas import tpu as pltpu

F32 = jnp.float32
BF16 = jnp.bfloat16
I32 = jnp.int32
U32 = jnp.uint32

HEAD_DIM = 128
GRID_W = 64
POOL_WINDOWS = (2, 4, 8, 16)
N_ADA = 6
TOP_K = 2
EPS = 1e-6

LANES = 128
ROW_TILE = 256
HGRN_CHUNK = 128
HGRN_MEMBERS = 2
EXPERT_TILE = 1024
HGRN_FAST_LIMIT = 160.0
VMEM_LIMIT_BYTES = 48 * 1024 * 1024

_NT = (((1,), (1,)), ((), ()))


def _params(*semantics):
    return pltpu.CompilerParams(dimension_semantics=semantics, vmem_limit_bytes=VMEM_LIMIT_BYTES)


def _dot(a, b):
    return jnp.dot(a, b, preferred_element_type=F32)


def _silu(x):
    return x * jax.nn.sigmoid(x)


def _rms(x, w):
    return x * lax.rsqrt(jnp.mean(x * x, axis=-1, keepdims=True) + EPS) * w


def _mod_row(tile, tiles_per_batch, ctx_tiles, batch):
    return jnp.where(tile % tiles_per_batch < ctx_tiles, batch, tile // tiles_per_batch)


def _mod_kernel(c_ref, w_ref, b_ref, o_ref):
    o_ref[0] = jnp.dot(_silu(c_ref[...]), w_ref[0], preferred_element_type=F32,
                       precision=lax.Precision.HIGHEST) + b_ref[0]


def _modulation(cond, w_ada, b_ada):
    depth, d, n_out = w_ada.shape
    rows = cond.shape[0]
    tn = 1024
    return pl.pallas_call(
        _mod_kernel,
        grid=(depth, n_out // tn),
        in_specs=[
            pl.BlockSpec((rows, d), lambda l, j: (0, 0)),
            pl.BlockSpec((1, d, tn), lambda l, j: (l, 0, j)),
            pl.BlockSpec((1, 1, tn), lambda l, j: (l, 0, j)),
        ],
        out_specs=pl.BlockSpec((1, rows, tn), lambda l, j: (l, 0, j)),
        out_shape=jax.ShapeDtypeStruct((depth, rows, n_out), F32),
        compiler_params=_params("arbitrary", "arbitrary"),
        name="modulation",
    )(cond, w_ada, b_ada.reshape(depth, 1, n_out))


def _stream_specs(h, geo):
    if not isinstance(h, tuple):
        return [pl.BlockSpec((ROW_TILE, h.shape[1]), lambda i: (i, 0))], [h]
    tpb, ct, batch = geo["tpb"], geo["ctx_tiles"], geo["batch"]
    lat = tpb - ct
    d = h[0].shape[1]

    def ctx_block(i):
        b, j = i // tpb, i % tpb
        return (jnp.where(j < ct, b * ct + j, jnp.minimum(b + 1, batch - 1) * ct), 0)

    def lat_block(i):
        b, j = i // tpb, i % tpb
        return (jnp.where(j < ct, b * lat, b * lat + j - ct), 0)

    return [pl.BlockSpec((ROW_TILE, d), ctx_block), pl.BlockSpec((ROW_TILE, d), lat_block)], list(h)


def _stream_tile(refs, geo):
    if len(refs) == 1:
        return refs[0][...]
    is_ctx = pl.program_id(0) % geo["tpb"] < geo["ctx_tiles"]
    return jnp.where(is_ctx, refs[0][...], refs[1][...])


def _mix_in_kernel(*refs, geo):
    mod_ref, nw_ref, w_ref, z_ref = refs[-4:]
    u = _rms(_stream_tile(refs[:-4], geo), nw_ref[...]) * (1 + mod_ref[0, 1:2, :]) + mod_ref[0, 0:1, :]
    z_ref[...] = _dot(u.astype(BF16), w_ref[...])


def _mix_in(h, mods, norm_w, w_in, geo):
    d, d_in = w_in.shape
    n = geo["batch"] * geo["seq"]
    row = functools.partial(_mod_row, tiles_per_batch=geo["tpb"], ctx_tiles=geo["ctx_tiles"], batch=geo["batch"])
    stream_specs, stream = _stream_specs(h, geo)
    return pl.pallas_call(
        functools.partial(_mix_in_kernel, geo=geo),
        grid=(n // ROW_TILE,),
        in_specs=stream_specs + [
            pl.BlockSpec((1, N_ADA, d), lambda i: (row(i), 0, 0)),
            pl.BlockSpec((1, d), lambda i: (0, 0)),
            pl.BlockSpec((d, d_in), lambda i: (0, 0)),
        ],
        out_specs=pl.BlockSpec((ROW_TILE, d_in), lambda i: (i, 0)),
        out_shape=jax.ShapeDtypeStruct((n, d_in), F32),
        compiler_params=_params("arbitrary"),
        name="mix_in",
    )(*stream, mods, norm_w.reshape(1, d), w_in)


def _split3(x):
    hi = x.astype(BF16)
    r1 = x - hi.astype(F32)
    mid = r1.astype(BF16)
    lo = (r1 - mid.astype(F32)).astype(BF16)
    return hi, mid, lo


def _chunk_cumsum(lg_ref, cum_ref, *, c, reverse):
    n = lg_ref.shape[0]
    t_idx = lax.broadcasted_iota(I32, (n, n), 0)
    s_idx = lax.broadcasted_iota(I32, (n, n), 1)
    incl = (s_idx >= t_idx) if reverse else (s_idx <= t_idx)
    same_chunk = (t_idx // c) == (s_idx // c)
    tri = jnp.where(jnp.logical_and(incl, same_chunk), 1.0, 0.0).astype(BF16)
    hi, mid, lo = _split3(lg_ref[...])
    cum_ref[...] = _dot(tri, hi) + _dot(tri, mid) + _dot(tri, lo)


def _hgrn_chunk(q_ref, v_ref, cum_ref, k_ref, o_ref, st_ref, row0, *, c, reverse, fast, heads):
    rows = pl.ds(row0, c)
    t_idx = lax.broadcasted_iota(I32, (c, c), 0)
    s_idx = lax.broadcasted_iota(I32, (c, c), 1)
    incl = (s_idx >= t_idx) if reverse else (s_idx <= t_idx)
    b = cum_ref[rows, :]
    b_end = b[0:1, :] if reverse else b[c - 1:c, :]
    q = q_ref[rows, :] * (HEAD_DIM ** -0.5)
    k = k_ref[rows, :]
    v = v_ref[rows, :]

    if fast:
        r = 0.5 * b_end
        er = jnp.exp(r)
        qt = (q * jnp.exp(b - r)).astype(BF16)
        kt = (k * jnp.exp(r - b)).astype(BF16)
        for h in range(heads):
            hs = slice(h * HEAD_DIM, (h + 1) * HEAD_DIM)
            scores = lax.dot_general(qt[:, hs], kt[:, hs], _NT, preferred_element_type=F32)
            p = jnp.where(incl, scores, 0.0).astype(BF16)
            st = st_ref[h] * er[:, hs]
            o = _dot(p, v[:, hs].astype(BF16)) + lax.dot_general(
                qt[:, hs], st.astype(BF16), _NT, preferred_element_type=F32)
            o_ref[rows, hs] = o
            st_ref[h] = (st + _dot(v[:, hs].T.astype(BF16), kt[:, hs])) * er[:, hs]
    else:
        qd = (q * jnp.exp(b)).astype(BF16)
        ke = (k * jnp.exp(b_end - b)).astype(BF16)
        decay = jnp.exp(b_end)
        for h in range(heads):
            hs = slice(h * HEAD_DIM, (h + 1) * HEAD_DIM)
            st = st_ref[h]
            o_ref[rows, hs] = lax.dot_general(qd[:, hs], st.astype(BF16), _NT, preferred_element_type=F32)
            st_ref[h] = st * decay[:, hs] + _dot(v[:, hs].T.astype(BF16), ke[:, hs])
        t_col = lax.broadcasted_iota(I32, (c, HEAD_DIM), 0)

        def key_step(s, carry):
            b_s = cum_ref[pl.ds(row0 + s, 1), :]
            k_s = k_ref[pl.ds(row0 + s, 1), :]
            v_s = v_ref[pl.ds(row0 + s, 1), :]
            prod = q * jnp.exp(jnp.minimum(b - b_s, 0.0)) * k_s
            visible = (t_col <= s) if reverse else (t_col >= s)
            for h in range(heads):
                hs = slice(h * HEAD_DIM, (h + 1) * HEAD_DIM)
                w = jnp.sum(prod[:, hs], axis=-1, keepdims=True)
                o_ref[rows, hs] += jnp.where(visible, w, 0.0) * v_s[:, hs]
            return carry

        lax.fori_loop(0, c, key_step, 0)


def _hgrn_kernel(qf_ref, vf_ref, ff_ref, qb_ref, vb_ref, fb_ref, lb_ref, of_ref, ob_ref,
                 stf_ref, stb_ref, lgf_ref, lgb_ref, kf_ref, kb_ref, cumf_ref, cumb_ref, *, heads):
    members, block = qf_ref.shape[0], qf_ref.shape[1]
    half = HGRN_CHUNK // 2
    n_half = block // half

    @pl.when(pl.program_id(1) == 0)
    def _():
        stf_ref[...] = jnp.zeros_like(stf_ref)
        stb_ref[...] = jnp.zeros_like(stb_ref)

    worst_full = jnp.float32(0.0)
    worst_half = jnp.float32(0.0)
    for m in range(members):
        for d, (f_ref, lg_ref, k_ref) in enumerate(((ff_ref, lgf_ref, kf_ref), (fb_ref, lgb_ref, kb_ref))):
            lb = lb_ref[d:d + 1, :]
            f = lb + (1 - lb) * jax.nn.sigmoid(f_ref[m])
            lg = jnp.log(f)
            lg_ref[m] = lg
            k_ref[m] = 1 - f
            half_decay = -jnp.sum(lg.reshape(n_half, half, lg.shape[-1]), axis=1)
            worst_half = jnp.maximum(worst_half, jnp.max(half_decay))
            for i in range(0, n_half, 2):
                worst_full = jnp.maximum(worst_full, jnp.max(half_decay[i:i + 1] + half_decay[i + 1:i + 2]))
    full_ok = worst_full < HGRN_FAST_LIMIT
    half_ok = jnp.logical_and(jnp.logical_not(full_ok), worst_half < HGRN_FAST_LIMIT)
    neither = jnp.logical_and(jnp.logical_not(full_ok), jnp.logical_not(worst_half < HGRN_FAST_LIMIT))

    def run(c, fast):
        n_chunks = block // c
        for m in range(members):
            _chunk_cumsum(lgf_ref.at[m], cumf_ref.at[m], c=c, reverse=False)
            _chunk_cumsum(lgb_ref.at[m], cumb_ref.at[m], c=c, reverse=True)

        def body(ci, carry):
            fwd_row = ci * c
            bwd_row = (n_chunks - 1 - ci) * c
            if not isinstance(ci, int):
                fwd_row, bwd_row = pl.multiple_of(fwd_row, c), pl.multiple_of(bwd_row, c)
            for m in range(members):
                _hgrn_chunk(qf_ref.at[m], vf_ref.at[m], cumf_ref.at[m], kf_ref.at[m], of_ref.at[m],
                            stf_ref.at[m], fwd_row, c=c, reverse=False, fast=fast, heads=heads)
                _hgrn_chunk(qb_ref.at[m], vb_ref.at[m], cumb_ref.at[m], kb_ref.at[m], ob_ref.at[m],
                            stb_ref.at[m], bwd_row, c=c, reverse=True, fast=fast, heads=heads)
            return carry

        if fast:
            for ci in range(n_chunks):
                body(ci, 0)
        else:
            lax.fori_loop(0, n_chunks, body, 0)

    @pl.when(full_ok)
    def _():
        run(HGRN_CHUNK, True)

    @pl.when(half_ok)
    def _():
        run(half, True)

    @pl.when(neither)
    def _():
        run(HGRN_CHUNK, False)


def _hgrn(z, lower_bounds, geo):
    n = z.shape[0]
    d_h = lower_bounds.shape[-1]
    heads = d_h // HEAD_DIM
    nblk, ctx_blocks, batch, seq = geo["tpb"], geo["ctx_tiles"], geo["batch"], geo["seq"]
    members = HGRN_MEMBERS if batch % HGRN_MEMBERS == 0 else 1
    z3 = z.reshape(batch, seq, z.shape[1])

    def fwd_block(j):
        return j

    def bwd_block(j):
        return jnp.where(j < ctx_blocks, ctx_blocks - 1 - j, nblk - 1 - (j - ctx_blocks))

    def col(block_fn, c):
        return pl.BlockSpec((members, ROW_TILE, d_h), lambda b, j: (b, block_fn(j), c))

    state = pltpu.VMEM((members, heads, HEAD_DIM, HEAD_DIM), F32)
    rows = pltpu.VMEM((members, ROW_TILE, d_h), F32)
    o_fwd, o_bwd = pl.pallas_call(
        functools.partial(_hgrn_kernel, heads=heads),
        grid=(batch // members, nblk),
        in_specs=[col(fwd_block, 0), col(fwd_block, 1), col(fwd_block, 2),
                  col(bwd_block, 0), col(bwd_block, 1), col(bwd_block, 3),
                  pl.BlockSpec((2, d_h), lambda b, j: (0, 0))],
        out_specs=[col(fwd_block, 0), col(bwd_block, 0)],
        out_shape=[jax.ShapeDtypeStruct((batch, seq, d_h), F32)] * 2,
        scratch_shapes=[state, state, rows, rows, rows, rows, rows, rows],
        compiler_params=_params("arbitrary", "arbitrary"),
        name="hgrn",
    )(z3, z3, z3, z3, z3, z3, lower_bounds)
    return o_fwd.reshape(n, d_h), o_bwd.reshape(n, d_h)


def _window_sum(x, k):
    n = x.shape[0]
    t = lax.broadcasted_iota(I32, x.shape, 0)

    def ahead(a, d):
        return jnp.where(t + d < n, pltpu.roll(a, n - d, axis=0), 0.0)

    def behind(a, d):
        return jnp.where(t >= d, pltpu.roll(a, d, axis=0), 0.0)

    fwd = x
    bwd = behind(x, 1)
    w = 1
    while 2 * w <= k // 2:
        fwd = fwd + ahead(fwd, w)
        bwd = bwd + behind(bwd, w)
        w *= 2
    return fwd + bwd


def _window_count(shape, n, k, offset=0):
    t = lax.broadcasted_iota(I32, shape, 0) + offset
    lo, hi = k // 2, k - 1 - k // 2
    return (jnp.minimum(t + hi + 1, n) - jnp.maximum(t - lo, 0)).astype(F32)


def _pool_group(x_ref, o_ref, cs_ref, k, ctx_len, grid_rows):
    lo, hi = k // 2, k - 1 - k // 2
    pad = POOL_WINDOWS[-1] // 2

    x = x_ref[0:ctx_len, :]
    mean = _window_sum(x, k) / _window_count(x.shape, ctx_len, k)
    o_ref[0:ctx_len, :] = (mean - x).astype(o_ref.dtype)

    zeros = jnp.zeros((pad * GRID_W, LANES), F32)
    cs_ref[0:pad * GRID_W, :] = zeros
    cs_ref[(pad + grid_rows) * GRID_W:(2 * pad + grid_rows) * GRID_W, :] = zeros

    def col_pass(r, carry):
        src = pl.multiple_of(ctx_len + r * GRID_W, GRID_W)
        dst = pl.multiple_of((pad + r) * GRID_W, GRID_W)
        cs_ref[pl.ds(dst, GRID_W), :] = _window_sum(x_ref[pl.ds(src, GRID_W), :], k)
        return carry

    lax.fori_loop(0, grid_rows, col_pass, 0)
    n_col = _window_count((GRID_W, LANES), GRID_W, k)

    def row_pass(r, carry):
        acc = jnp.zeros((GRID_W, LANES), F32)
        for dr in range(-lo, hi + 1):
            acc = acc + cs_ref[pl.ds(pl.multiple_of((pad + r + dr) * GRID_W, GRID_W), GRID_W), :]
        n_row = (jnp.minimum(r + hi + 1, grid_rows) - jnp.maximum(r - lo, 0)).astype(F32)
        src = pl.multiple_of(ctx_len + r * GRID_W, GRID_W)
        o_ref[pl.ds(src, GRID_W), :] = (acc / (n_row * n_col) - x_ref[pl.ds(src, GRID_W), :]).astype(o_ref.dtype)
        return carry

    lax.fori_loop(0, grid_rows, row_pass, 0)


def _pool_kernel(x_ref, o_ref, cs_ref, *, ctx_len, grid_rows):
    group = pl.program_id(1)
    for gi, k in enumerate(POOL_WINDOWS):
        @pl.when(group == gi)
        def _(k=k):
            _pool_group(x_ref, o_ref, cs_ref, k, ctx_len, grid_rows)


def _pool(z, geo):
    n, d_in = z.shape
    groups = len(POOL_WINDOWS)
    first = d_in // LANES - groups
    seq, ctx_len = geo["seq"], geo["ctx_len"]
    grid_rows = (seq - ctx_len) // GRID_W
    pad = POOL_WINDOWS[-1] // 2
    return pl.pallas_call(
        functools.partial(_pool_kernel, ctx_len=ctx_len, grid_rows=grid_rows),
        grid=(geo["batch"], groups),
        in_specs=[pl.BlockSpec((seq, LANES), lambda b, g: (b, first + g))],
        out_specs=pl.BlockSpec((seq, LANES), lambda b, g: (b, g)),
        out_shape=jax.ShapeDtypeStruct((n, groups * LANES), BF16),
        scratch_shapes=[pltpu.VMEM(((grid_rows + 2 * pad) * GRID_W, LANES), F32)],
        compiler_params=_params("arbitrary", "arbitrary"),
        name="pool",
    )(z)


def _pack_bf16_pairs(x):
    w = x.shape[1] // 2
    lo = lax.bitcast_convert_type(x[:, :w].astype(BF16).astype(F32), U32)
    hi = lax.bitcast_convert_type(x[:, w:].astype(BF16).astype(F32), U32)
    return (lo >> 16) | (hi & jnp.uint32(0xFFFF0000))


def _unpack_bf16_pairs(w):
    lo = lax.bitcast_convert_type(w << 16, F32).astype(BF16)
    hi = lax.bitcast_convert_type(w & jnp.uint32(0xFFFF0000), F32).astype(BF16)
    return lo, hi


ROUTE_POS_LANE = 6


def _route(v, rw_ref, carry_ref, n_exp, cap):
    rows = v.shape[0]
    v_hi = v.astype(BF16)
    v_lo = (v - v_hi.astype(F32)).astype(BF16)
    hi_prod = _dot(v_hi, rw_ref[...])
    logits = hi_prod[:, :LANES] + hi_prod[:, LANES:] + _dot(v_lo, rw_ref[:, :LANES])
    lane = lax.broadcasted_iota(I32, (rows, LANES), 1).astype(F32)
    neg = jnp.float32(-jnp.inf)
    logits = jnp.where(lane < n_exp, logits, neg)
    m1 = jnp.max(logits, axis=-1, keepdims=True)
    i1 = jnp.min(jnp.where(logits == m1, lane, float(LANES)), axis=-1, keepdims=True)
    rest = jnp.where(lane == i1, neg, logits)
    m2 = jnp.max(rest, axis=-1, keepdims=True)
    i2 = jnp.min(jnp.where(rest == m2, lane, float(LANES)), axis=-1, keepdims=True)
    e = jnp.exp(m2 - m1)
    w1 = 1.0 / (1.0 + e)
    w2 = e / (1.0 + e)
    chosen = jnp.where((lane == i1) | (lane == i2), 1.0, 0.0)
    t_idx = lax.broadcasted_iota(I32, (rows, rows), 0)
    s_idx = lax.broadcasted_iota(I32, (rows, rows), 1)
    before = jnp.where(s_idx < t_idx, 1.0, 0.0).astype(BF16)
    slots = _dot(before, chosen.astype(BF16)) + carry_ref[...]
    r1 = jnp.sum(jnp.where(lane == i1, slots, 0.0), axis=-1, keepdims=True)
    r2 = jnp.sum(jnp.where(lane == i2, slots, 0.0), axis=-1, keepdims=True)
    carry_ref[...] += jnp.sum(chosen, axis=0, keepdims=True)
    out = jnp.zeros((rows, LANES), F32)
    for idx, val in enumerate((i1, i2, r1, r2, w1, w2, i1 * cap + r1, i2 * cap + r2)):
        out = jnp.where(lane == idx, val, out)
    return out


def _mixer_tail(of_ref, ob_ref, g_ref, pd_ref, h, mod_ref, hnw_ref, wp_ref, ps_ref, wo_ref, nw_ref, *,
                heads, groups):
    d_h = heads * HEAD_DIM
    o = of_ref[...] + ob_ref[...]
    normed = []
    for hd in range(heads):
        oh = o[:, hd * HEAD_DIM:(hd + 1) * HEAD_DIM]
        normed.append(oh * lax.rsqrt(jnp.mean(oh * oh, axis=-1, keepdims=True) + EPS))
    a = jnp.concatenate(normed, axis=-1) * hnw_ref[...] * _silu(g_ref[...])
    y = _dot(a.astype(BF16), wo_ref[0:d_h, :])
    for gi in range(groups):
        gs = slice(gi * LANES, (gi + 1) * LANES)
        p = _dot(pd_ref[:, gs], wp_ref[gi]) * ps_ref[:, gs]
        y = y + _dot(p.astype(BF16), wo_ref[d_h + gi * LANES:d_h + (gi + 1) * LANES, :])
    h1 = h + mod_ref[0, 2:3, :] * y
    v = _rms(h1, nw_ref[...]) * (1 + mod_ref[0, 4:5, :]) + mod_ref[0, 3:4, :]
    return h1, v


def _mix_out_kernel(*refs, heads, groups, geo):
    h1_ref, v_ref = refs[-2:]
    stream_refs = refs[4:-8]
    h1, v = _mixer_tail(*refs[:4], _stream_tile(stream_refs, geo), *refs[-8:-2], heads=heads, groups=groups)
    h1_ref[...] = h1
    v_ref[...] = v.astype(BF16)


def _mix_out_route_kernel(*refs, heads, groups, n_exp, cap):
    rw_ref, h1_ref, route_ref, cnt_ref, xs_ref = refs[11:16]
    carry_ref, vp0, vp1, pv0, pv1, ps0, ps1, pos_sem, row_sem = refs[16:]
    vp, pos_vmem, pos_smem = (vp0, vp1), (pv0, pv1), (ps0, ps1)
    rows = vp0.shape[0]
    i = pl.program_id(0)
    n_tiles = pl.num_programs(0) - 1

    def pos_copy(slot):
        return pltpu.make_async_copy(pos_vmem[slot], pos_smem[slot], pos_sem)

    def scatter(slot):
        pos_copy(slot).wait()
        for r in range(rows):
            for k in range(TOP_K):
                pltpu.make_async_copy(vp[slot].at[pl.ds(r, 1), :],
                                      xs_ref.at[pl.ds(pos_smem[slot][ROUTE_POS_LANE + k, r], 1), :],
                                      row_sem).start(priority=k)

    def scatter_wait(slot):
        for _ in range(TOP_K):
            pltpu.make_async_copy(vp[slot], xs_ref.at[pl.ds(0, rows), :], row_sem).wait()

    def compute(slot):
        h1, v = _mixer_tail(*refs[:4], refs[4][...], *refs[5:11], heads=heads, groups=groups)
        h1_ref[...] = h1
        route =_route(v, rw_ref, carry_ref, n_exp, cap)
        route_ref[...] = route
        cnt_ref[...] = jnp.broadcast_to(carry_ref[...], cnt_ref.shape)
        vp[slot][...] = _pack_bf16_pairs(v)
        pos_vmem[slot][...] = route.T[0:pos_vmem[slot].shape[0], :].astype(I32)
        pos_copy(slot).start()

    @pl.when(i == 0)
    def _():
        carry_ref[...] = jnp.zeros_like(carry_ref)
        compute(0)

    for slot in (0, 1):
        @pl.when((i > 0) & (i < n_tiles) & (i % 2 == slot))
        def _(slot=slot):
            scatter(1 - slot)
            compute(slot)
            scatter_wait(1 - slot)

        @pl.when((i == n_tiles) & (i % 2 == slot))
        def _(slot=slot):
            scatter(1 - slot)
            scatter_wait(1 - slot)


def _mix_out(o_fwd, o_bwd, z, pdiff, h, mods, hgrn_norm_w, w_pool, pool_scale, w_out, norm_w, router_w, geo):
    n, d = o_fwd.shape[0], w_out.shape[1]
    d_h = o_fwd.shape[1]
    groups = w_pool.shape[0]
    d_p = groups * LANES
    n_tiles = n // ROW_TILE
    routed = router_w is not None
    clamp = (lambda i: jnp.minimum(i, n_tiles - 1)) if routed else (lambda i: i)
    row = functools.partial(_mod_row, tiles_per_batch=geo["tpb"], ctx_tiles=geo["ctx_tiles"], batch=geo["batch"])
    tile = lambda width, c=0: pl.BlockSpec((ROW_TILE, width), lambda i: (clamp(i), c))
    whole = lambda shape: pl.BlockSpec(shape, lambda i: (0,) * len(shape))
    stream_specs, stream = ([tile(d)], [h]) if routed else _stream_specs(h, geo)
    in_specs = [tile(d_h), tile(d_h), tile(d_h, 4), tile(d_p), *stream_specs,
                pl.BlockSpec((1, N_ADA, d), lambda i: (row(clamp(i)), 0, 0)),
                whole((1, d_h)), whole((groups, LANES, LANES)), whole((1, d_p)), whole((d_h + d_p, d)),
                whole((1, d))]
    args = [o_fwd, o_bwd, z, pdiff, *stream, mods, hgrn_norm_w.reshape(1, d_h), w_pool,
            pool_scale.reshape(1, d_p), w_out, norm_w.reshape(1, d)]
    heads = d_h // HEAD_DIM
    if not routed:
        return pl.pallas_call(
            functools.partial(_mix_out_kernel, heads=heads, groups=groups, geo=geo),
            grid=(n_tiles,),
            in_specs=in_specs,
            out_specs=[tile(d), tile(d)],
            out_shape=[jax.ShapeDtypeStruct((n, d), F32), jax.ShapeDtypeStruct((n, d), BF16)],
            compiler_params=_params("arbitrary"),
            name="mix_out",
        )(*args)

    n_exp = router_w.shape[1]
    cap = _expert_capacity(n)
    rw = jnp.pad(router_w.astype(F32), ((0, 0), (0, LANES - n_exp)))
    rw_hi = rw.astype(BF16)
    args.append(jnp.concatenate([rw_hi, (rw - rw_hi.astype(F32)).astype(BF16)], axis=1))
    in_specs.append(whole((d, 2 * LANES)))
    packed = pltpu.VMEM((ROW_TILE, d // 2), U32)
    return pl.pallas_call(
        functools.partial(_mix_out_route_kernel, heads=heads, groups=groups, n_exp=n_exp, cap=cap),
        grid=(n_tiles + 1,),
        in_specs=in_specs,
        out_specs=[tile(d), tile(LANES), whole((8, LANES)), pl.BlockSpec(memory_space=pl.ANY)],
        out_shape=[jax.ShapeDtypeStruct((n, d), F32), jax.ShapeDtypeStruct((n, LANES), F32),
                   jax.ShapeDtypeStruct((8, LANES), F32), jax.ShapeDtypeStruct((n_exp * cap, d // 2), U32)],
        scratch_shapes=[pltpu.VMEM((1, LANES), F32), packed, packed,
                        pltpu.VMEM((8, ROW_TILE), I32), pltpu.VMEM((8, ROW_TILE), I32),
                        pltpu.SMEM((8, ROW_TILE), I32), pltpu.SMEM((8, ROW_TILE), I32),
                        pltpu.SemaphoreType.DMA(()), pltpu.SemaphoreType.DMA(())],
        compiler_params=_params("arbitrary"),
        name="mix_out_route",
    )(*args)


def _expert_capacity(n_tokens):
    return -(-n_tokens // EXPERT_TILE) * EXPERT_TILE


def _residual_out(h_ref, mod_ref, y, fnw_ref, out_ref):
    h2 = h_ref[...] + mod_ref[0, 5:6, :] * y
    out_ref[...] = h2 if fnw_ref is None else _rms(h2, fnw_ref[...])


def _ffn_kernel(*refs, final):
    x_ref, h_ref, mod_ref, wg_ref, wu_ref, wd_ref = refs[:6]
    fnw_ref = refs[6] if final else None
    out_ref = refs[-1]
    x = x_ref[...]
    a = (_silu(_dot(x, wg_ref[...])) * _dot(x, wu_ref[...])).astype(BF16)
    _residual_out(h_ref, mod_ref, _dot(a, wd_ref[...]), fnw_ref, out_ref)


def _tile_maps(geo, final):
    tpb, ctx_tiles, batch = geo["tpb"], geo["ctx_tiles"], geo["batch"]
    if final:
        lat = tpb - ctx_tiles
        return (batch, lat), (lambda b, j: b * tpb + ctx_tiles + j), (lambda b, j: b * lat + j), batch * lat * ROW_TILE
    return (batch, tpb), (lambda b, j: b * tpb + j), (lambda b, j: b * tpb + j), batch * tpb * ROW_TILE


def _ffn(v, h, mods, wg, wu, wd, final_norm_w, geo):
    n, d = h.shape
    d_ff = wg.shape[1]
    final = final_norm_w is not None
    grid, in_tile, out_tile, out_rows = _tile_maps(geo, final)
    whole = lambda shape: pl.BlockSpec(shape, lambda b, j: (0,) * len(shape))
    in_specs = [pl.BlockSpec((ROW_TILE, d), lambda b, j: (in_tile(b, j), 0)),
                pl.BlockSpec((ROW_TILE, d), lambda b, j: (in_tile(b, j), 0)),
                pl.BlockSpec((1, N_ADA, d), lambda b, j: (b, 0, 0)) if final else
                pl.BlockSpec((1, N_ADA, d), lambda b, j: (jnp.where(j < geo["ctx_tiles"], geo["batch"], b), 0, 0)),
                whole((d, d_ff)), whole((d, d_ff)), whole((d_ff, d))]
    args = [v, h, mods, wg, wu, wd]
    if final:
        in_specs.append(whole((1, d)))
        args.append(final_norm_w.reshape(1, d))
    return pl.pallas_call(
        functools.partial(_ffn_kernel, final=final),
        grid=grid,
        in_specs=in_specs,
        out_specs=pl.BlockSpec((ROW_TILE, d), lambda b, j: (out_tile(b, j), 0)),
        out_shape=jax.ShapeDtypeStruct((out_rows, d), F32),
        compiler_params=_params("arbitrary", "arbitrary"),
        name="ffn",
    )(*args)


def _zero_padding_kernel(cnt_ref, xs_in_ref, xs_ref, zero_ref, sem, *, cap):
    del xs_in_ref
    zero_ref[...] = jnp.zeros_like(zero_ref)

    def row_copy(e, r):
        return pltpu.make_async_copy(zero_ref.at[pl.ds(0, 1), :], xs_ref.at[pl.ds(e * cap + r, 1), :], sem)

    for e in range(cnt_ref.shape[0]):
        used = cnt_ref[e]
        tile_end = (used + EXPERT_TILE - 1) // EXPERT_TILE * EXPERT_TILE
        lax.fori_loop(used, tile_end, lambda r, c, e=e: (row_copy(e, r).start(), c)[1], 0)
        lax.fori_loop(used, tile_end, lambda r, c, e=e: (row_copy(e, r).wait(), c)[1], 0)


def _zero_padding(xs, cnt, cap):
    grid_spec = pltpu.PrefetchScalarGridSpec(
        num_scalar_prefetch=1,
        grid=(1,),
        in_specs=[pl.BlockSpec(memory_space=pl.ANY)],
        out_specs=pl.BlockSpec(memory_space=pl.ANY),
        scratch_shapes=[pltpu.VMEM((8, xs.shape[1]), U32), pltpu.SemaphoreType.DMA(())],
    )
    return pl.pallas_call(
        functools.partial(_zero_padding_kernel, cap=cap),
        grid_spec=grid_spec,
        out_shape=jax.ShapeDtypeStruct(xs.shape, xs.dtype),
        input_output_aliases={1: 0},
        compiler_params=_params("arbitrary"),
        name="zero_padding",
    )(cnt, xs)


def _experts_kernel(te_ref, tb_ref, na_ref, xs_ref, wg_ref, wu_ref, wd_ref, y_ref, acc_ref, xlo_ref, xhi_ref):
    t, f = pl.program_id(0), pl.program_id(1)
    last = pl.num_programs(1) - 1
    half = xlo_ref.shape[1]
    active = t < na_ref[0]

    @pl.when(active & (f == 0))
    def _():
        xlo_ref[...], xhi_ref[...] = _unpack_bf16_pairs(xs_ref[...])
        acc_ref[...] = jnp.zeros_like(acc_ref)

    @pl.when(active)
    def _():
        xlo, xhi = xlo_ref[...], xhi_ref[...]
        g = _dot(xlo, wg_ref[0, 0:half, :]) + _dot(xhi, wg_ref[0, half:, :])
        u = _dot(xlo, wu_ref[0, 0:half, :]) + _dot(xhi, wu_ref[0, half:, :])
        acc_ref[...] += _dot((_silu(g) * u).astype(BF16), wd_ref[0])

    @pl.when(active & (f == last))
    def _():
        y_ref[...] = acc_ref[...]

    @pl.when(jnp.logical_not(active) & (f == last))
    def _():
        y_ref[...] = jnp.zeros_like(y_ref)


def _experts(xs, tile_expert, tile_block, n_active, n_tiles, wg, wu, wd):
    rows, half = xs.shape
    _, d, d_ff = wg.shape
    tf = 512
    n_f = d_ff // tf
    spare = rows // EXPERT_TILE

    def live(t, na):
        return jnp.minimum(t, na[0] - 1)

    def up(t, f, te, tb, na):
        return (te[live(t, na)], 0, jnp.where(t < na[0], f, n_f - 1))

    def down(t, f, te, tb, na):
        return (te[live(t, na)], jnp.where(t < na[0], f, n_f - 1), 0)

    grid_spec = pltpu.PrefetchScalarGridSpec(
        num_scalar_prefetch=3,
        grid=(n_tiles, n_f),
        in_specs=[pl.BlockSpec((EXPERT_TILE, half), lambda t, f, te, tb, na: (tb[live(t, na)], 0)),
                  pl.BlockSpec((1, d, tf), up), pl.BlockSpec((1, d, tf), up), pl.BlockSpec((1, tf, d), down)],
        out_specs=pl.BlockSpec((EXPERT_TILE, d),
                               lambda t, f, te, tb, na: (jnp.where(t < na[0], tb[t], spare), 0)),
        scratch_shapes=[pltpu.VMEM((EXPERT_TILE, d), F32), pltpu.VMEM((EXPERT_TILE, half), BF16),
                        pltpu.VMEM((EXPERT_TILE, half), BF16)],
    )
    return pl.pallas_call(
        _experts_kernel,
        grid_spec=grid_spec,
        out_shape=jax.ShapeDtypeStruct((rows + EXPERT_TILE, d), F32),
        compiler_params=_params("arbitrary", "arbitrary"),
        name="experts",
    )(tile_expert, tile_block, n_active, xs, wg, wu, wd)


def _combine_kernel(*refs, in_tile, final):
    pos_ref, h_ref, route_ref, mod_ref = refs[:4]
    fnw_ref = refs[4] if final else None
    y_ref, out_ref, ybuf0, ybuf1, marker_ref, sem0, sem1, marker_sem = refs[-8:]
    ybuf, sems = (ybuf0, ybuf1), (sem0, sem1)
    inner = pl.num_programs(1)
    step = pl.program_id(0) * inner + pl.program_id(1)
    n_steps = pl.num_programs(0) * inner
    rows = h_ref.shape[0]

    def gather(s, slot):
        tile = in_tile(s // inner, s % inner)
        for r in range(rows):
            for k in range(TOP_K):
                pltpu.make_async_copy(y_ref.at[pl.ds(pos_ref[tile, TOP_K * r + k], 1), :],
                                      ybuf[slot].at[k, pl.ds(r, 1), :], sems[slot]).start(priority=k)

    def gather_wait(slot):
        for k in range(TOP_K):
            pltpu.make_async_copy(y_ref.at[pl.ds(0, rows), :], ybuf[slot].at[k], sems[slot]).wait()

    @pl.when(step == 0)
    def _():
        marker_ref[...] = jnp.zeros_like(marker_ref)
        gather(step, 0)

    for slot in (0, 1):
        @pl.when(step % 2 == slot)
        def _(slot=slot):
            gather_wait(slot)

            @pl.when(step + 1 < n_steps)
            def _():
                marker = pltpu.make_async_copy(marker_ref.at[0], marker_ref.at[1], marker_sem)
                marker.start()
                gather(step + 1, 1 - slot)
                marker.wait()

            y = route_ref[:, 4:5] * ybuf[slot][0] + route_ref[:, 5:6] * ybuf[slot][1]
            _residual_out(h_ref, mod_ref, y, fnw_ref, out_ref)


def _combine(y_sorted, pos, route, h, mods, final_norm_w, geo):
    n, d = h.shape
    final = final_norm_w is not None
    grid, in_tile, out_tile, out_rows = _tile_maps(geo, final)
    in_specs = [pl.BlockSpec((ROW_TILE, d), lambda b, j, pos: (in_tile(b, j), 0)),
                pl.BlockSpec((ROW_TILE, LANES), lambda b, j, pos: (in_tile(b, j), 0)),
                pl.BlockSpec((1, N_ADA, d), lambda b, j, pos: (b, 0, 0)) if final else
                pl.BlockSpec((1, N_ADA, d),
                             lambda b, j, pos: (jnp.where(j < geo["ctx_tiles"], geo["batch"], b), 0, 0))]
    args = [h, route, mods]
    if final:
        in_specs.append(pl.BlockSpec((1, d), lambda b, j, pos: (0, 0)))
        args.append(final_norm_w.reshape(1, d))
    in_specs.append(pl.BlockSpec(memory_space=pl.ANY))
    args.append(y_sorted)
    grid_spec = pltpu.PrefetchScalarGridSpec(
        num_scalar_prefetch=1,
        grid=grid,
        in_specs=in_specs,
        out_specs=pl.BlockSpec((ROW_TILE, d), lambda b, j, pos: (out_tile(b, j), 0)),
        scratch_shapes=[pltpu.VMEM((TOP_K, ROW_TILE, d), F32), pltpu.VMEM((TOP_K, ROW_TILE, d), F32),
                        pltpu.VMEM((2, 8, LANES), I32),
                        pltpu.SemaphoreType.DMA(()), pltpu.SemaphoreType.DMA(()), pltpu.SemaphoreType.DMA(())],
    )
    return pl.pallas_call(
        functools.partial(_combine_kernel, in_tile=in_tile, final=final),
        grid_spec=grid_spec,
        out_shape=jax.ShapeDtypeStruct((out_rows, d), F32),
        compiler_params=_params("arbitrary", "arbitrary"),
        name="combine",
    )(pos, *args)


def _tile_schedule(cnt, n_tiles, cap):
    n_exp = cnt.shape[0]
    tiles = (cnt + EXPERT_TILE - 1) // EXPERT_TILE
    ends = jnp.cumsum(tiles)
    t = jnp.arange(n_tiles, dtype=I32)
    expert = jnp.minimum(jnp.sum(t[:, None] >= ends[None, :], axis=-1), n_exp - 1).astype(I32)
    first = jnp.sum(jnp.where(expert[:, None] == jnp.arange(n_exp, dtype=I32), ends - tiles, 0), axis=-1)
    block = expert * (cap // EXPERT_TILE) + (t - first)
    return expert, block.astype(I32), ends[-1:].astype(I32)


def _moe(xs, route, counts, h, mods, wg, wu, wd, first_expert, n_exp, final_norm_w, geo):
    n = h.shape[0]
    cap = _expert_capacity(n)
    n_tiles = -(-n * TOP_K // EXPERT_TILE) + n_exp
    cnt = counts[0, :n_exp].astype(I32)
    tile_expert, tile_block, n_active = _tile_schedule(cnt, n_tiles, cap)
    xs = _zero_padding(xs, cnt, cap)
    ys = _experts(xs, tile_expert + first_expert, tile_block, n_active, n_tiles, wg, wu, wd)
    pos = route[:, ROUTE_POS_LANE:ROUTE_POS_LANE + TOP_K].astype(I32).reshape(-1, TOP_K * ROW_TILE)
    return _combine(ys, pos, route, h, mods, final_norm_w, geo)


def kernel(x, c, ctx, c_ctx, w_ada, b_ada, norm_w, w_in, hgrn_lb_raw, hgrn_norm_w, w_pool, pool_scale, w_out,
           ffn_wg, ffn_wu, ffn_wd, router_w, moe_wg, moe_wu, moe_wd, final_norm_w):
    batch, seq_len, d = x.shape
    ctx_len = ctx.shape[1]
    depth = w_in.shape[0]
    seq = ctx_len + seq_len
    assert ctx_len % ROW_TILE == 0 and seq_len % ROW_TILE == 0 and seq_len % GRID_W == 0
    geo = dict(batch=batch, seq=seq, ctx_len=ctx_len, tpb=seq // ROW_TILE, ctx_tiles=ctx_len // ROW_TILE)

    lb_cum = jnp.cumsum(jax.nn.softmax(hgrn_lb_raw.astype(F32), axis=0), axis=0)
    lower_bounds = lb_cum - lb_cum[:1]

    cond_rows = -(-(batch + 1) // 8) * 8
    cond = jnp.concatenate([c, c_ctx[None, :], jnp.zeros((cond_rows - batch - 1, d), F32)], axis=0)
    mods_all = _modulation(cond, w_ada, b_ada).reshape(depth, cond_rows, N_ADA, d)

    n_exp = moe_wg.shape[1]
    moe_w = [w.astype(BF16).reshape((-1,) + w.shape[2:]) for w in (moe_wg, moe_wu, moe_wd)]

    h = (ctx.reshape(batch * ctx_len, d), x.reshape(batch * seq_len, d))
    for l in range(depth):
        mods = mods_all[l]
        final_w = final_norm_w if l == depth - 1 else None
        z = _mix_in(h, mods, norm_w[l, 0], w_in[l].astype(BF16), geo)
        o_fwd, o_bwd = _hgrn(z, lower_bounds[l], geo)
        pdiff = _pool(z, geo)
        dense = l % 2 == 0
        outs = _mix_out(o_fwd, o_bwd, z, pdiff, h, mods, hgrn_norm_w[l], w_pool[l].astype(BF16), pool_scale[l],
                        w_out[l].astype(BF16), norm_w[l, 1], None if dense else router_w[l // 2], geo)
        if dense:
            h1, v = outs
            h = _ffn(v, h1, mods, ffn_wg[l // 2].astype(BF16), ffn_wu[l // 2].astype(BF16),
                     ffn_wd[l // 2].astype(BF16), final_w, geo)
        else:
            h1, route, counts, xs = outs
            h = _moe(xs, route, counts, h1, mods, *moe_w, (l // 2) * n_exp, n_exp, final_w, geo)
    return h.reshape(batch, seq_len, d)
```

```python
import functools

import jax
import jax.numpy as jnp
from jax import lax
from jax.experimental import pallas as pl
from jax.experimental.pallas import tpu as pltpu

F32 = jnp.float32
BF16 = jnp.bfloat16
I32 = jnp.int32
U32 = jnp.uint32

HEAD_DIM = 128
GRID_W = 64
POOL_WINDOWS = (2, 4, 8, 16)
N_ADA = 6
TOP_K = 2
EPS = 1e-6

LANES = 128
ROW_TILE = 256
HGRN_CHUNK = 128
HGRN_MEMBERS = 2
EXPERT_TILE = 1024
HGRN_FAST_LIMIT = 160.0
VMEM_LIMIT_BYTES = 48 * 1024 * 1024

_NT = (((1,), (1,)), ((), ()))


def _params(*semantics):
    return pltpu.CompilerParams(dimension_semantics=semantics, vmem_limit_bytes=VMEM_LIMIT_BYTES)


def _dot(a, b):
    return jnp.dot(a, b, preferred_element_type=F32)


def _silu(x):
    return x * jax.nn.sigmoid(x)


def _rms(x, w):
    return x * lax.rsqrt(jnp.mean(x * x, axis=-1, keepdims=True) + EPS) * w


def _mod_row(tile, tiles_per_batch, ctx_tiles, batch):
    return jnp.where(tile % tiles_per_batch < ctx_tiles, batch, tile // tiles_per_batch)


def _mod_kernel(c_ref, w_ref, b_ref, o_ref):
    o_ref[0] = jnp.dot(_silu(c_ref[...]), w_ref[0], preferred_element_type=F32,
                       precision=lax.Precision.HIGHEST) + b_ref[0]


def _modulation(cond, w_ada, b_ada):
    depth, d, n_out = w_ada.shape
    rows = cond.shape[0]
    tn = 1024
    return pl.pallas_call(
        _mod_kernel,
        grid=(depth, n_out // tn),
        in_specs=[
            pl.BlockSpec((rows, d), lambda l, j: (0, 0)),
            pl.BlockSpec((1, d, tn), lambda l, j: (l, 0, j)),
            pl.BlockSpec((1, 1, tn), lambda l, j: (l, 0, j)),
        ],
        out_specs=pl.BlockSpec((1, rows, tn), lambda l, j: (l, 0, j)),
        out_shape=jax.ShapeDtypeStruct((depth, rows, n_out), F32),
        compiler_params=_params("arbitrary", "arbitrary"),
        name="modulation",
    )(cond, w_ada, b_ada.reshape(depth, 1, n_out))


def _stream_specs(h, geo):
    if not isinstance(h, tuple):
        return [pl.BlockSpec((ROW_TILE, h.shape[1]), lambda i: (i, 0))], [h]
    tpb, ct, batch = geo["tpb"], geo["ctx_tiles"], geo["batch"]
    lat = tpb - ct
    d = h[0].shape[1]

    def ctx_block(i):
        b, j = i // tpb, i % tpb
        return (jnp.where(j < ct, b * ct + j, jnp.minimum(b + 1, batch - 1) * ct), 0)

    def lat_block(i):
        b, j = i // tpb, i % tpb
        return (jnp.where(j < ct, b * lat, b * lat + j - ct), 0)

    return [pl.BlockSpec((ROW_TILE, d), ctx_block), pl.BlockSpec((ROW_TILE, d), lat_block)], list(h)


def _stream_tile(refs, geo):
    if len(refs) == 1:
        return refs[0][...]
    is_ctx = pl.program_id(0) % geo["tpb"] < geo["ctx_tiles"]
    return jnp.where(is_ctx, refs[0][...], refs[1][...])


def _mix_in_kernel(*refs, geo):
    mod_ref, nw_ref, w_ref, z_ref = refs[-4:]
    u = _rms(_stream_tile(refs[:-4], geo), nw_ref[...]) * (1 + mod_ref[0, 1:2, :]) + mod_ref[0, 0:1, :]
    z_ref[...] = _dot(u.astype(BF16), w_ref[...])


def _mix_in(h, mods, norm_w, w_in, geo):
    d, d_in = w_in.shape
    n = geo["batch"] * geo["seq"]
    row = functools.partial(_mod_row, tiles_per_batch=geo["tpb"], ctx_tiles=geo["ctx_tiles"], batch=geo["batch"])
    stream_specs, stream = _stream_specs(h, geo)
    return pl.pallas_call(
        functools.partial(_mix_in_kernel, geo=geo),
        grid=(n // ROW_TILE,),
        in_specs=stream_specs + [
            pl.BlockSpec((1, N_ADA, d), lambda i: (row(i), 0, 0)),
            pl.BlockSpec((1, d), lambda i: (0, 0)),
            pl.BlockSpec((d, d_in), lambda i: (0, 0)),
        ],
        out_specs=pl.BlockSpec((ROW_TILE, d_in), lambda i: (i, 0)),
        out_shape=jax.ShapeDtypeStruct((n, d_in), F32),
        compiler_params=_params("arbitrary"),
        name="mix_in",
    )(*stream, mods, norm_w.reshape(1, d), w_in)


def _split3(x):
    hi = x.astype(BF16)
    r1 = x - hi.astype(F32)
    mid = r1.astype(BF16)
    lo = (r1 - mid.astype(F32)).astype(BF16)
    return hi, mid, lo


def _chunk_cumsum(lg_ref, cum_ref, *, c, reverse):
    n = lg_ref.shape[0]
    t_idx = lax.broadcasted_iota(I32, (n, n), 0)
    s_idx = lax.broadcasted_iota(I32, (n, n), 1)
    incl = (s_idx >= t_idx) if reverse else (s_idx <= t_idx)
    same_chunk = (t_idx // c) == (s_idx // c)
    tri = jnp.where(jnp.logical_and(incl, same_chunk), 1.0, 0.0).astype(BF16)
    hi, mid, _ = _split3(lg_ref[...])
    cum_ref[...] = _dot(tri, hi) + _dot(tri, mid)


def _hgrn_chunk(q_ref, v_ref, cum_ref, k_ref, o_ref, st_ref, row0, *, c, reverse, fast, heads):
    rows = pl.ds(row0, c)
    t_idx = lax.broadcasted_iota(I32, (c, c), 0)
    s_idx = lax.broadcasted_iota(I32, (c, c), 1)
    incl = (s_idx >= t_idx) if reverse else (s_idx <= t_idx)
    b = cum_ref[rows, :]
    b_end = b[0:1, :] if reverse else b[c - 1:c, :]
    q = q_ref[rows, :] * (HEAD_DIM ** -0.5)
    k = k_ref[rows, :]
    v = v_ref[rows, :]

    if fast:
        r = 0.5 * b_end
        er = jnp.exp(r)
        qt = (q * jnp.exp(b - r)).astype(BF16)
        kt = (k * jnp.exp(r - b)).astype(BF16)
        for h in range(heads):
            hs = slice(h * HEAD_DIM, (h + 1) * HEAD_DIM)
            scores = lax.dot_general(qt[:, hs], kt[:, hs], _NT, preferred_element_type=F32)
            p = jnp.where(incl, scores, 0.0).astype(BF16)
            st = st_ref[h] * er[:, hs]
            o = _dot(p, v[:, hs].astype(BF16)) + lax.dot_general(
                qt[:, hs], st.astype(BF16), _NT, preferred_element_type=F32)
            o_ref[rows, hs] = o
            st_ref[h] = (st + _dot(v[:, hs].T.astype(BF16), kt[:, hs])) * er[:, hs]
    else:
        qd = (q * jnp.exp(b)).astype(BF16)
        ke = (k * jnp.exp(b_end - b)).astype(BF16)
        decay = jnp.exp(b_end)
        for h in range(heads):
            hs = slice(h * HEAD_DIM, (h + 1) * HEAD_DIM)
            st = st_ref[h]
            o_ref[rows, hs] = lax.dot_general(qd[:, hs], st.astype(BF16), _NT, preferred_element_type=F32)
            st_ref[h] = st * decay[:, hs] + _dot(v[:, hs].T.astype(BF16), ke[:, hs])
        t_col = lax.broadcasted_iota(I32, (c, HEAD_DIM), 0)

        def key_step(s, carry):
            b_s = cum_ref[pl.ds(row0 + s, 1), :]
            k_s = k_ref[pl.ds(row0 + s, 1), :]
            v_s = v_ref[pl.ds(row0 + s, 1), :]
            prod = q * jnp.exp(jnp.minimum(b - b_s, 0.0)) * k_s
            visible = (t_col <= s) if reverse else (t_col >= s)
            for h in range(heads):
                hs = slice(h * HEAD_DIM, (h + 1) * HEAD_DIM)
                w = jnp.sum(prod[:, hs], axis=-1, keepdims=True)
                o_ref[rows, hs] += jnp.where(visible, w, 0.0) * v_s[:, hs]
            return carry

        lax.fori_loop(0, c, key_step, 0)


def _hgrn_kernel(qf_ref, vf_ref, ff_ref, qb_ref, vb_ref, fb_ref, lb_ref, of_ref, ob_ref,
                 stf_ref, stb_ref, lgf_ref, lgb_ref, kf_ref, kb_ref, cumf_ref, cumb_ref, *, heads):
    members, block = qf_ref.shape[0], qf_ref.shape[1]
    half = HGRN_CHUNK // 2
    n_half = block // half

    @pl.when(pl.program_id(1) == 0)
    def _():
        stf_ref[...] = jnp.zeros_like(stf_ref)
        stb_ref[...] = jnp.zeros_like(stb_ref)

    worst_full = jnp.float32(0.0)
    worst_half = jnp.float32(0.0)
    for m in range(members):
        for d, (f_ref, lg_ref, k_ref) in enumerate(((ff_ref, lgf_ref, kf_ref), (fb_ref, lgb_ref, kb_ref))):
            lb = lb_ref[d:d + 1, :]
            f = lb + (1 - lb) * jax.nn.sigmoid(f_ref[m])
            lg = jnp.log(f)
            lg_ref[m] = lg
            k_ref[m] = 1 - f
            half_decay = -jnp.sum(lg.reshape(n_half, half, lg.shape[-1]), axis=1)
            worst_half = jnp.maximum(worst_half, jnp.max(half_decay))
            for i in range(0, n_half, 2):
                worst_full = jnp.maximum(worst_full, jnp.max(half_decay[i:i + 1] + half_decay[i + 1:i + 2]))
    full_ok = worst_full < HGRN_FAST_LIMIT
    half_ok = jnp.logical_and(jnp.logical_not(full_ok), worst_half < HGRN_FAST_LIMIT)
    neither = jnp.logical_and(jnp.logical_not(full_ok), jnp.logical_not(worst_half < HGRN_FAST_LIMIT))

    def run(c, fast):
        n_chunks = block // c
        for m in range(members):
            _chunk_cumsum(lgf_ref.at[m], cumf_ref.at[m], c=c, reverse=False)
            _chunk_cumsum(lgb_ref.at[m], cumb_ref.at[m], c=c, reverse=True)

        def body(ci, carry):
            fwd_row = ci * c
            bwd_row = (n_chunks - 1 - ci) * c
            if not isinstance(ci, int):
                fwd_row, bwd_row = pl.multiple_of(fwd_row, c), pl.multiple_of(bwd_row, c)
            for m in range(members):
                _hgrn_chunk(qf_ref.at[m], vf_ref.at[m], cumf_ref.at[m], kf_ref.at[m], of_ref.at[m],
                            stf_ref.at[m], fwd_row, c=c, reverse=False, fast=fast, heads=heads)
                _hgrn_chunk(qb_ref.at[m], vb_ref.at[m], cumb_ref.at[m], kb_ref.at[m], ob_ref.at[m],
                            stb_ref.at[m], bwd_row, c=c, reverse=True, fast=fast, heads=heads)
            return carry

        if fast:
            for ci in range(n_chunks):
                body(ci, 0)
        else:
            lax.fori_loop(0, n_chunks, body, 0)

    @pl.when(full_ok)
    def _():
        run(HGRN_CHUNK, True)

    @pl.when(half_ok)
    def _():
        run(half, True)

    @pl.when(neither)
    def _():
        run(HGRN_CHUNK, False)


def _hgrn(z, lower_bounds, geo):
    n = z.shape[0]
    d_h = lower_bounds.shape[-1]
    heads = d_h // HEAD_DIM
    nblk, ctx_blocks, batch, seq = geo["tpb"], geo["ctx_tiles"], geo["batch"], geo["seq"]
    members = HGRN_MEMBERS if batch % HGRN_MEMBERS == 0 else 1
    z3 = z.reshape(batch, seq, z.shape[1])

    def fwd_block(j):
        return j

    def bwd_block(j):
        return jnp.where(j < ctx_blocks, ctx_blocks - 1 - j, nblk - 1 - (j - ctx_blocks))

    def col(block_fn, c):
        return pl.BlockSpec((members, ROW_TILE, d_h), lambda b, j: (b, block_fn(j), c))

    state = pltpu.VMEM((members, heads, HEAD_DIM, HEAD_DIM), F32)
    rows = pltpu.VMEM((members, ROW_TILE, d_h), F32)
    o_fwd, o_bwd = pl.pallas_call(
        functools.partial(_hgrn_kernel, heads=heads),
        grid=(batch // members, nblk),
        in_specs=[col(fwd_block, 0), col(fwd_block, 1), col(fwd_block, 2),
                  col(bwd_block, 0), col(bwd_block, 1), col(bwd_block, 3),
                  pl.BlockSpec((2, d_h), lambda b, j: (0, 0))],
        out_specs=[col(fwd_block, 0), col(bwd_block, 0)],
        out_shape=[jax.ShapeDtypeStruct((batch, seq, d_h), F32)] * 2,
        scratch_shapes=[state, state, rows, rows, rows, rows, rows, rows],
        compiler_params=_params("arbitrary", "arbitrary"),
        name="hgrn",
    )(z3, z3, z3, z3, z3, z3, lower_bounds)
    return o_fwd.reshape(n, d_h), o_bwd.reshape(n, d_h)


def _window_sum(x, k):
    n = x.shape[0]
    t = lax.broadcasted_iota(I32, x.shape, 0)

    def ahead(a, d):
        return jnp.where(t + d < n, pltpu.roll(a, n - d, axis=0), 0.0)

    def behind(a, d):
        return jnp.where(t >= d, pltpu.roll(a, d, axis=0), 0.0)

    fwd = x
    bwd = behind(x, 1)
    w = 1
    while 2 * w <= k // 2:
        fwd = fwd + ahead(fwd, w)
        bwd = bwd + behind(bwd, w)
        w *= 2
    return fwd + bwd


def _window_count(shape, n, k, offset=0):
    t = lax.broadcasted_iota(I32, shape, 0) + offset
    lo, hi = k // 2, k - 1 - k // 2
    return (jnp.minimum(t + hi + 1, n) - jnp.maximum(t - lo, 0)).astype(F32)


def _pool_group(x_ref, o_ref, cs_ref, k, ctx_len, grid_rows):
    lo, hi = k // 2, k - 1 - k // 2
    pad = POOL_WINDOWS[-1] // 2

    x = x_ref[0:ctx_len, :]
    mean = _window_sum(x, k) / _window_count(x.shape, ctx_len, k)
    o_ref[0:ctx_len, :] = (mean - x).astype(o_ref.dtype)

    zeros = jnp.zeros((pad * GRID_W, LANES), F32)
    cs_ref[0:pad * GRID_W, :] = zeros
    cs_ref[(pad + grid_rows) * GRID_W:(2 * pad + grid_rows) * GRID_W, :] = zeros

    def col_pass(r, carry):
        src = pl.multiple_of(ctx_len + r * GRID_W, GRID_W)
        dst = pl.multiple_of((pad + r) * GRID_W, GRID_W)
        cs_ref[pl.ds(dst, GRID_W), :] = _window_sum(x_ref[pl.ds(src, GRID_W), :], k)
        return carry

    lax.fori_loop(0, grid_rows, col_pass, 0)
    n_col = _window_count((GRID_W, LANES), GRID_W, k)

    def row_pass(r, carry):
        acc = jnp.zeros((GRID_W, LANES), F32)
        for dr in range(-lo, hi + 1):
            acc = acc + cs_ref[pl.ds(pl.multiple_of((pad + r + dr) * GRID_W, GRID_W), GRID_W), :]
        n_row = (jnp.minimum(r + hi + 1, grid_rows) - jnp.maximum(r - lo, 0)).astype(F32)
        src = pl.multiple_of(ctx_len + r * GRID_W, GRID_W)
        o_ref[pl.ds(src, GRID_W), :] = (acc / (n_row * n_col) - x_ref[pl.ds(src, GRID_W), :]).astype(o_ref.dtype)
        return carry

    lax.fori_loop(0, grid_rows, row_pass, 0)


def _pool_kernel(x_ref, o_ref, cs_ref, *, ctx_len, grid_rows):
    group = pl.program_id(1)
    for gi, k in enumerate(POOL_WINDOWS):
        @pl.when(group == gi)
        def _(k=k):
            _pool_group(x_ref, o_ref, cs_ref, k, ctx_len, grid_rows)


def _pool(z, geo):
    n, d_in = z.shape
    groups = len(POOL_WINDOWS)
    first = d_in // LANES - groups
    seq, ctx_len = geo["seq"], geo["ctx_len"]
    grid_rows = (seq - ctx_len) // GRID_W
    pad = POOL_WINDOWS[-1] // 2
    return pl.pallas_call(
        functools.partial(_pool_kernel, ctx_len=ctx_len, grid_rows=grid_rows),
        grid=(geo["batch"], groups),
        in_specs=[pl.BlockSpec((seq, LANES), lambda b, g: (b, first + g))],
        out_specs=pl.BlockSpec((seq, LANES), lambda b, g: (b, g)),
        out_shape=jax.ShapeDtypeStruct((n, groups * LANES), BF16),
        scratch_shapes=[pltpu.VMEM(((grid_rows + 2 * pad) * GRID_W, LANES), F32)],
        compiler_params=_params("arbitrary", "arbitrary"),
        name="pool",
    )(z)


def _pack_bf16_pairs(x):
    w = x.shape[1] // 2
    lo = lax.bitcast_convert_type(x[:, :w].astype(BF16).astype(F32), U32)
    hi = lax.bitcast_convert_type(x[:, w:].astype(BF16).astype(F32), U32)
    return (lo >> 16) | (hi & jnp.uint32(0xFFFF0000))


def _unpack_bf16_pairs(w):
    lo = lax.bitcast_convert_type(w << 16, F32).astype(BF16)
    hi = lax.bitcast_convert_type(w & jnp.uint32(0xFFFF0000), F32).astype(BF16)
    return lo, hi


ROUTE_POS_LANE = 6


def _route(v, rw_ref, carry_ref, n_exp, cap):
    rows = v.shape[0]
    v_hi = v.astype(BF16)
    v_lo = (v - v_hi.astype(F32)).astype(BF16)
    hi_prod = _dot(v_hi, rw_ref[...])
    logits = hi_prod[:, :LANES] + hi_prod[:, LANES:] + _dot(v_lo, rw_ref[:, :LANES])
    lane = lax.broadcasted_iota(I32, (rows, LANES), 1).astype(F32)
    neg = jnp.float32(-jnp.inf)
    logits = jnp.where(lane < n_exp, logits, neg)
    m1 = jnp.max(logits, axis=-1, keepdims=True)
    i1 = jnp.min(jnp.where(logits == m1, lane, float(LANES)), axis=-1, keepdims=True)
    rest = jnp.where(lane == i1, neg, logits)
    m2 = jnp.max(rest, axis=-1, keepdims=True)
    i2 = jnp.min(jnp.where(rest == m2, lane, float(LANES)), axis=-1, keepdims=True)
    e = jnp.exp(m2 - m1)
    w1 = 1.0 / (1.0 + e)
    w2 = e / (1.0 + e)
    chosen = jnp.where((lane == i1) | (lane == i2), 1.0, 0.0)
    t_idx = lax.broadcasted_iota(I32, (rows, rows), 0)
    s_idx = lax.broadcasted_iota(I32, (rows, rows), 1)
    before = jnp.where(s_idx < t_idx, 1.0, 0.0).astype(BF16)
    slots = _dot(before, chosen.astype(BF16)) + carry_ref[...]
    r1 = jnp.sum(jnp.where(lane == i1, slots, 0.0), axis=-1, keepdims=True)
    r2 = jnp.sum(jnp.where(lane == i2, slots, 0.0), axis=-1, keepdims=True)
    carry_ref[...] += jnp.sum(chosen, axis=0, keepdims=True)
    out = jnp.zeros((rows, LANES), F32)
    for idx, val in enumerate((i1, i2, r1, r2, w1, w2, i1 * cap + r1, i2 * cap + r2)):
        out = jnp.where(lane == idx, val, out)
    return out


def _mixer_tail(of_ref, ob_ref, g_ref, pd_ref, h, mod_ref, hnw_ref, wp_ref, ps_ref, wo_ref, nw_ref, *,
                heads, groups):
    d_h = heads * HEAD_DIM
    o = of_ref[...] + ob_ref[...]
    normed = []
    for hd in range(heads):
        oh = o[:, hd * HEAD_DIM:(hd + 1) * HEAD_DIM]
        normed.append(oh * lax.rsqrt(jnp.mean(oh * oh, axis=-1, keepdims=True) + EPS))
    a = jnp.concatenate(normed, axis=-1) * hnw_ref[...] * _silu(g_ref[...])
    y = _dot(a.astype(BF16), wo_ref[0:d_h, :])
    for gi in range(groups):
        gs = slice(gi * LANES, (gi + 1) * LANES)
        p = _dot(pd_ref[:, gs], wp_ref[gi]) * ps_ref[:, gs]
        y = y + _dot(p.astype(BF16), wo_ref[d_h + gi * LANES:d_h + (gi + 1) * LANES, :])
    h1 = h + mod_ref[0, 2:3, :] * y
    v = _rms(h1, nw_ref[...]) * (1 + mod_ref[0, 4:5, :]) + mod_ref[0, 3:4, :]
    return h1, v


def _mix_out_kernel(*refs, heads, groups, geo):
    h1_ref, v_ref = refs[-2:]
    stream_refs = refs[4:-8]
    h1, v = _mixer_tail(*refs[:4], _stream_tile(stream_refs, geo), *refs[-8:-2], heads=heads, groups=groups)
    h1_ref[...] = h1
    v_ref[...] = v.astype(BF16)


def _mix_out_route_kernel(*refs, heads, groups, n_exp, cap):
    rw_ref, h1_ref, route_ref, cnt_ref, xs_ref = refs[11:16]
    carry_ref, vp0, vp1, pv0, pv1, ps0, ps1, pos_sem, row_sem = refs[16:]
    vp, pos_vmem, pos_smem = (vp0, vp1), (pv0, pv1), (ps0, ps1)
    rows = vp0.shape[0]
    i = pl.program_id(0)
    n_tiles = pl.num_programs(0) - 1

    def pos_copy(slot):
        return pltpu.make_async_copy(pos_vmem[slot], pos_smem[slot], pos_sem)

    def scatter(slot):
        pos_copy(slot).wait()
        for r in range(rows):
            for k in range(TOP_K):
                pltpu.make_async_copy(vp[slot].at[pl.ds(r, 1), :],
                                      xs_ref.at[pl.ds(pos_smem[slot][ROUTE_POS_LANE + k, r], 1), :],
                                      row_sem).start(priority=k)

    def scatter_wait(slot):
        for _ in range(TOP_K):
            pltpu.make_async_copy(vp[slot], xs_ref.at[pl.ds(0, rows), :], row_sem).wait()

    def compute(slot):
        h1, v = _mixer_tail(*refs[:4], refs[4][...], *refs[5:11], heads=heads, groups=groups)
        h1_ref[...] = h1
        route =_route(v, rw_ref, carry_ref, n_exp, cap)
        route_ref[...] = route
        cnt_ref[...] = jnp.broadcast_to(carry_ref[...], cnt_ref.shape)
        vp[slot][...] = _pack_bf16_pairs(v)
        pos_vmem[slot][...] = route.T[0:pos_vmem[slot].shape[0], :].astype(I32)
        pos_copy(slot).start()

    @pl.when(i == 0)
    def _():
        carry_ref[...] = jnp.zeros_like(carry_ref)
        compute(0)

    for slot in (0, 1):
        @pl.when((i > 0) & (i < n_tiles) & (i % 2 == slot))
        def _(slot=slot):
            scatter(1 - slot)
            compute(slot)
            scatter_wait(1 - slot)

        @pl.when((i == n_tiles) & (i % 2 == slot))
        def _(slot=slot):
            scatter(1 - slot)
            scatter_wait(1 - slot)


def _mix_out(o_fwd, o_bwd, z, pdiff, h, mods, hgrn_norm_w, w_pool, pool_scale, w_out, norm_w, router_w, geo):
    n, d = o_fwd.shape[0], w_out.shape[1]
    d_h = o_fwd.shape[1]
    groups = w_pool.shape[0]
    d_p = groups * LANES
    n_tiles = n // ROW_TILE
    routed = router_w is not None
    clamp = (lambda i: jnp.minimum(i, n_tiles - 1)) if routed else (lambda i: i)
    row = functools.partial(_mod_row, tiles_per_batch=geo["tpb"], ctx_tiles=geo["ctx_tiles"], batch=geo["batch"])
    tile = lambda width, c=0: pl.BlockSpec((ROW_TILE, width), lambda i: (clamp(i), c))
    whole = lambda shape: pl.BlockSpec(shape, lambda i: (0,) * len(shape))
    stream_specs, stream = ([tile(d)], [h]) if routed else _stream_specs(h, geo)
    in_specs = [tile(d_h), tile(d_h), tile(d_h, 4), tile(d_p), *stream_specs,
                pl.BlockSpec((1, N_ADA, d), lambda i: (row(clamp(i)), 0, 0)),
                whole((1, d_h)), whole((groups, LANES, LANES)), whole((1, d_p)), whole((d_h + d_p, d)),
                whole((1, d))]
    args = [o_fwd, o_bwd, z, pdiff, *stream, mods, hgrn_norm_w.reshape(1, d_h), w_pool,
            pool_scale.reshape(1, d_p), w_out, norm_w.reshape(1, d)]
    heads = d_h // HEAD_DIM
    if not routed:
        return pl.pallas_call(
            functools.partial(_mix_out_kernel, heads=heads, groups=groups, geo=geo),
            grid=(n_tiles,),
            in_specs=in_specs,
            out_specs=[tile(d), tile(d)],
            out_shape=[jax.ShapeDtypeStruct((n, d), F32), jax.ShapeDtypeStruct((n, d), BF16)],
            compiler_params=_params("arbitrary"),
            name="mix_out",
        )(*args)

    n_exp = router_w.shape[1]
    cap = _expert_capacity(n)
    rw = jnp.pad(router_w.astype(F32), ((0, 0), (0, LANES - n_exp)))
    rw_hi = rw.astype(BF16)
    args.append(jnp.concatenate([rw_hi, (rw - rw_hi.astype(F32)).astype(BF16)], axis=1))
    in_specs.append(whole((d, 2 * LANES)))
    packed = pltpu.VMEM((ROW_TILE, d // 2), U32)
    return pl.pallas_call(
        functools.partial(_mix_out_route_kernel, heads=heads, groups=groups, n_exp=n_exp, cap=cap),
        grid=(n_tiles + 1,),
        in_specs=in_specs,
        out_specs=[tile(d), tile(LANES), whole((8, LANES)), pl.BlockSpec(memory_space=pl.ANY)],
        out_shape=[jax.ShapeDtypeStruct((n, d), F32), jax.ShapeDtypeStruct((n, LANES), F32),
                   jax.ShapeDtypeStruct((8, LANES), F32), jax.ShapeDtypeStruct((n_exp * cap, d // 2), U32)],
        scratch_shapes=[pltpu.VMEM((1, LANES), F32), packed, packed,
                        pltpu.VMEM((8, ROW_TILE), I32), pltpu.VMEM((8, ROW_TILE), I32),
                        pltpu.SMEM((8, ROW_TILE), I32), pltpu.SMEM((8, ROW_TILE), I32),
                        pltpu.SemaphoreType.DMA(()), pltpu.SemaphoreType.DMA(())],
        compiler_params=_params("arbitrary"),
        name="mix_out_route",
    )(*args)


def _expert_capacity(n_tokens):
    return -(-n_tokens // EXPERT_TILE) * EXPERT_TILE


def _residual_out(h_ref, mod_ref, y, fnw_ref, out_ref):
    h2 = h_ref[...] + mod_ref[0, 5:6, :] * y
    out_ref[...] = h2 if fnw_ref is None else _rms(h2, fnw_ref[...])


def _ffn_kernel(*refs, final):
    x_ref, h_ref, mod_ref, wg_ref, wu_ref, wd_ref = refs[:6]
    fnw_ref = refs[6] if final else None
    out_ref = refs[-1]
    x = x_ref[...]
    a = (_silu(_dot(x, wg_ref[...])) * _dot(x, wu_ref[...])).astype(BF16)
    _residual_out(h_ref, mod_ref, _dot(a, wd_ref[...]), fnw_ref, out_ref)


def _tile_maps(geo, final):
    tpb, ctx_tiles, batch = geo["tpb"], geo["ctx_tiles"], geo["batch"]
    if final:
        lat = tpb - ctx_tiles
        return (batch, lat), (lambda b, j: b * tpb + ctx_tiles + j), (lambda b, j: b * lat + j), batch * lat * ROW_TILE
    return (batch, tpb), (lambda b, j: b * tpb + j), (lambda b, j: b * tpb + j), batch * tpb * ROW_TILE


def _ffn(v, h, mods, wg, wu, wd, final_norm_w, geo):
    n, d = h.shape
    d_ff = wg.shape[1]
    final = final_norm_w is not None
    grid, in_tile, out_tile, out_rows = _tile_maps(geo, final)
    whole = lambda shape: pl.BlockSpec(shape, lambda b, j: (0,) * len(shape))
    in_specs = [pl.BlockSpec((ROW_TILE, d), lambda b, j: (in_tile(b, j), 0)),
                pl.BlockSpec((ROW_TILE, d), lambda b, j: (in_tile(b, j), 0)),
                pl.BlockSpec((1, N_ADA, d), lambda b, j: (b, 0, 0)) if final else
                pl.BlockSpec((1, N_ADA, d), lambda b, j: (jnp.where(j < geo["ctx_tiles"], geo["batch"], b), 0, 0)),
                whole((d, d_ff)), whole((d, d_ff)), whole((d_ff, d))]
    args = [v, h, mods, wg, wu, wd]
    if final:
        in_specs.append(whole((1, d)))
        args.append(final_norm_w.reshape(1, d))
    return pl.pallas_call(
        functools.partial(_ffn_kernel, final=final),
        grid=grid,
        in_specs=in_specs,
        out_specs=pl.BlockSpec((ROW_TILE, d), lambda b, j: (out_tile(b, j), 0)),
        out_shape=jax.ShapeDtypeStruct((out_rows, d), F32),
        compiler_params=_params("arbitrary", "arbitrary"),
        name="ffn",
    )(*args)


def _zero_padding_kernel(cnt_ref, xs_in_ref, xs_ref, zero_ref, sem, *, cap):
    del xs_in_ref
    zero_ref[...] = jnp.zeros_like(zero_ref)

    def row_copy(e, r):
        return pltpu.make_async_copy(zero_ref.at[pl.ds(0, 1), :], xs_ref.at[pl.ds(e * cap + r, 1), :], sem)

    for e in range(cnt_ref.shape[0]):
        used = cnt_ref[e]
        tile_end = (used + EXPERT_TILE - 1) // EXPERT_TILE * EXPERT_TILE
        lax.fori_loop(used, tile_end, lambda r, c, e=e: (row_copy(e, r).start(), c)[1], 0)
        lax.fori_loop(used, tile_end, lambda r, c, e=e: (row_copy(e, r).wait(), c)[1], 0)


def _zero_padding(xs, cnt, cap):
    grid_spec = pltpu.PrefetchScalarGridSpec(
        num_scalar_prefetch=1,
        grid=(1,),
        in_specs=[pl.BlockSpec(memory_space=pl.ANY)],
        out_specs=pl.BlockSpec(memory_space=pl.ANY),
        scratch_shapes=[pltpu.VMEM((8, xs.shape[1]), U32), pltpu.SemaphoreType.DMA(())],
    )
    return pl.pallas_call(
        functools.partial(_zero_padding_kernel, cap=cap),
        grid_spec=grid_spec,
        out_shape=jax.ShapeDtypeStruct(xs.shape, xs.dtype),
        input_output_aliases={1: 0},
        compiler_params=_params("arbitrary"),
        name="zero_padding",
    )(cnt, xs)


def _experts_kernel(te_ref, tb_ref, na_ref, xs_ref, wg_ref, wu_ref, wd_ref, y_ref, acc_ref, xlo_ref, xhi_ref):
    t, f = pl.program_id(0), pl.program_id(1)
    last = pl.num_programs(1) - 1
    half = xlo_ref.shape[1]
    active = t < na_ref[0]

    @pl.when(active & (f == 0))
    def _():
        xlo_ref[...], xhi_ref[...] = _unpack_bf16_pairs(xs_ref[...])
        acc_ref[...] = jnp.zeros_like(acc_ref)

    @pl.when(active)
    def _():
        xlo, xhi = xlo_ref[...], xhi_ref[...]
        g = _dot(xlo, wg_ref[0, 0:half, :]) + _dot(xhi, wg_ref[0, half:, :])
        u = _dot(xlo, wu_ref[0, 0:half, :]) + _dot(xhi, wu_ref[0, half:, :])
        acc_ref[...] += _dot((_silu(g) * u).astype(BF16), wd_ref[0])

    @pl.when(active & (f == last))
    def _():
        y_ref[...] = acc_ref[...]

    @pl.when(jnp.logical_not(active) & (f == last))
    def _():
        y_ref[...] = jnp.zeros_like(y_ref)


def _experts(xs, tile_expert, tile_block, n_active, n_tiles, wg, wu, wd):
    rows, half = xs.shape
    _, d, d_ff = wg.shape
    tf = 512
    n_f = d_ff // tf
    spare = rows // EXPERT_TILE

    def live(t, na):
        return jnp.minimum(t, na[0] - 1)

    def up(t, f, te, tb, na):
        return (te[live(t, na)], 0, jnp.where(t < na[0], f, n_f - 1))

    def down(t, f, te, tb, na):
        return (te[live(t, na)], jnp.where(t < na[0], f, n_f - 1), 0)

    grid_spec = pltpu.PrefetchScalarGridSpec(
        num_scalar_prefetch=3,
        grid=(n_tiles, n_f),
        in_specs=[pl.BlockSpec((EXPERT_TILE, half), lambda t, f, te, tb, na: (tb[live(t, na)], 0)),
                  pl.BlockSpec((1, d, tf), up), pl.BlockSpec((1, d, tf), up), pl.BlockSpec((1, tf, d), down)],
        out_specs=pl.BlockSpec((EXPERT_TILE, d),
                               lambda t, f, te, tb, na: (jnp.where(t < na[0], tb[t], spare), 0)),
        scratch_shapes=[pltpu.VMEM((EXPERT_TILE, d), F32), pltpu.VMEM((EXPERT_TILE, half), BF16),
                        pltpu.VMEM((EXPERT_TILE, half), BF16)],
    )
    return pl.pallas_call(
        _experts_kernel,
        grid_spec=grid_spec,
        out_shape=jax.ShapeDtypeStruct((rows + EXPERT_TILE, d), F32),
        compiler_params=_params("arbitrary", "arbitrary"),
        name="experts",
    )(tile_expert, tile_block, n_active, xs, wg, wu, wd)


def _combine_kernel(*refs, in_tile, final):
    pos_ref, h_ref, route_ref, mod_ref = refs[:4]
    fnw_ref = refs[4] if final else None
    y_ref, out_ref, ybuf0, ybuf1, marker_ref, sem0, sem1, marker_sem = refs[-8:]
    ybuf, sems = (ybuf0, ybuf1), (sem0, sem1)
    inner = pl.num_programs(1)
    step = pl.program_id(0) * inner + pl.program_id(1)
    n_steps = pl.num_programs(0) * inner
    rows = h_ref.shape[0]

    def gather(s, slot):
        tile = in_tile(s // inner, s % inner)
        for r in range(rows):
            for k in range(TOP_K):
                pltpu.make_async_copy(y_ref.at[pl.ds(pos_ref[tile, TOP_K * r + k], 1), :],
                                      ybuf[slot].at[k, pl.ds(r, 1), :], sems[slot]).start(priority=k)

    def gather_wait(slot):
        for k in range(TOP_K):
            pltpu.make_async_copy(y_ref.at[pl.ds(0, rows), :], ybuf[slot].at[k], sems[slot]).wait()

    @pl.when(step == 0)
    def _():
        marker_ref[...] = jnp.zeros_like(marker_ref)
        gather(step, 0)

    for slot in (0, 1):
        @pl.when(step % 2 == slot)
        def _(slot=slot):
            gather_wait(slot)

            @pl.when(step + 1 < n_steps)
            def _():
                marker = pltpu.make_async_copy(marker_ref.at[0], marker_ref.at[1], marker_sem)
                marker.start()
                gather(step + 1, 1 - slot)
                marker.wait()

            y = route_ref[:, 4:5] * ybuf[slot][0] + route_ref[:, 5:6] * ybuf[slot][1]
            _residual_out(h_ref, mod_ref, y, fnw_ref, out_ref)


def _combine(y_sorted, pos, route, h, mods, final_norm_w, geo):
    n, d = h.shape
    final = final_norm_w is not None
    grid, in_tile, out_tile, out_rows = _tile_maps(geo, final)
    in_specs = [pl.BlockSpec((ROW_TILE, d), lambda b, j, pos: (in_tile(b, j), 0)),
                pl.BlockSpec((ROW_TILE, LANES), lambda b, j, pos: (in_tile(b, j), 0)),
                pl.BlockSpec((1, N_ADA, d), lambda b, j, pos: (b, 0, 0)) if final else
                pl.BlockSpec((1, N_ADA, d),
                             lambda b, j, pos: (jnp.where(j < geo["ctx_tiles"], geo["batch"], b), 0, 0))]
    args = [h, route, mods]
    if final:
        in_specs.append(pl.BlockSpec((1, d), lambda b, j, pos: (0, 0)))
        args.append(final_norm_w.reshape(1, d))
    in_specs.append(pl.BlockSpec(memory_space=pl.ANY))
    args.append(y_sorted)
    grid_spec = pltpu.PrefetchScalarGridSpec(
        num_scalar_prefetch=1,
        grid=grid,
        in_specs=in_specs,
        out_specs=pl.BlockSpec((ROW_TILE, d), lambda b, j, pos: (out_tile(b, j), 0)),
        scratch_shapes=[pltpu.VMEM((TOP_K, ROW_TILE, d), F32), pltpu.VMEM((TOP_K, ROW_TILE, d), F32),
                        pltpu.VMEM((2, 8, LANES), I32),
                        pltpu.SemaphoreType.DMA(()), pltpu.SemaphoreType.DMA(()), pltpu.SemaphoreType.DMA(())],
    )
    return pl.pallas_call(
        functools.partial(_combine_kernel, in_tile=in_tile, final=final),
        grid_spec=grid_spec,
        out_shape=jax.ShapeDtypeStruct((out_rows, d), F32),
        compiler_params=_params("arbitrary", "arbitrary"),
        name="combine",
    )(pos, *args)


def _tile_schedule(cnt, n_tiles, cap):
    n_exp = cnt.shape[0]
    tiles = (cnt + EXPERT_TILE - 1) // EXPERT_TILE
    ends = jnp.cumsum(tiles)
    t = jnp.arange(n_tiles, dtype=I32)
    expert = jnp.minimum(jnp.sum(t[:, None] >= ends[None, :], axis=-1), n_exp - 1).astype(I32)
    first = jnp.sum(jnp.where(expert[:, None] == jnp.arange(n_exp, dtype=I32), ends - tiles, 0), axis=-1)
    block = expert * (cap // EXPERT_TILE) + (t - first)
    return expert, block.astype(I32), ends[-1:].astype(I32)


def _moe(xs, route, counts, h, mods, wg, wu, wd, first_expert, n_exp, final_norm_w, geo):
    n = h.shape[0]
    cap = _expert_capacity(n)
    n_tiles = -(-n * TOP_K // EXPERT_TILE) + n_exp
    cnt = counts[0, :n_exp].astype(I32)
    tile_expert, tile_block, n_active = _tile_schedule(cnt, n_tiles, cap)
    xs = _zero_padding(xs, cnt, cap)
    ys = _experts(xs, tile_expert + first_expert, tile_block, n_active, n_tiles, wg, wu, wd)
    pos = route[:, ROUTE_POS_LANE:ROUTE_POS_LANE + TOP_K].astype(I32).reshape(-1, TOP_K * ROW_TILE)
    return _combine(ys, pos, route, h, mods, final_norm_w, geo)


def kernel(x, c, ctx, c_ctx, w_ada, b_ada, norm_w, w_in, hgrn_lb_raw, hgrn_norm_w, w_pool, pool_scale, w_out,
           ffn_wg, ffn_wu, ffn_wd, router_w, moe_wg, moe_wu, moe_wd, final_norm_w):
    batch, seq_len, d = x.shape
    ctx_len = ctx.shape[1]
    depth = w_in.shape[0]
    seq = ctx_len + seq_len
    assert ctx_len % ROW_TILE == 0 and seq_len % ROW_TILE == 0 and seq_len % GRID_W == 0
    geo = dict(batch=batch, seq=seq, ctx_len=ctx_len, tpb=seq // ROW_TILE, ctx_tiles=ctx_len // ROW_TILE)

    lb_cum = jnp.cumsum(jax.nn.softmax(hgrn_lb_raw.astype(F32), axis=0), axis=0)
    lower_bounds = lb_cum - lb_cum[:1]

    cond_rows = -(-(batch + 1) // 8) * 8
    cond = jnp.concatenate([c, c_ctx[None, :], jnp.zeros((cond_rows - batch - 1, d), F32)], axis=0)
    mods_all = _modulation(cond, w_ada, b_ada).reshape(depth, cond_rows, N_ADA, d)

    n_exp = moe_wg.shape[1]
    moe_w = [w.astype(BF16).reshape((-1,) + w.shape[2:]) for w in (moe_wg, moe_wu, moe_wd)]

    h = (ctx.reshape(batch * ctx_len, d), x.reshape(batch * seq_len, d))
    for l in range(depth):
        mods = mods_all[l]
        final_w = final_norm_w if l == depth - 1 else None
        z = _mix_in(h, mods, norm_w[l, 0], w_in[l].astype(BF16), geo)
        o_fwd, o_bwd = _hgrn(z, lower_bounds[l], geo)
        pdiff = _pool(z, geo)
        dense = l % 2 == 0
        outs = _mix_out(o_fwd, o_bwd, z, pdiff, h, mods, hgrn_norm_w[l], w_pool[l].astype(BF16), pool_scale[l],
                        w_out[l].astype(BF16), norm_w[l, 1], None if dense else router_w[l // 2], geo)
        if dense:
            h1, v = outs
            h = _ffn(v, h1, mods, ffn_wg[l // 2].astype(BF16), ffn_wu[l // 2].astype(BF16),
                     ffn_wd[l // 2].astype(BF16), final_w, geo)
        else:
            h1, route, counts, xs = outs
            h = _moe(xs, route, counts, h1, mods, *moe_w, (l // 2) * n_exp, n_exp, final_w, geo)
    return h.reshape(batch, seq_len, d)
```
